```python
import math
import jax, jax.numpy as jnp
from jax import lax
import numpy as np

D_MODEL = 4096
BATCH = 1
SEQ = 8192
DEPTH = 1

N_META = 16
SSM_HEADS = 64
SSM_HEAD_DIM = 64
D_SSM = SSM_HEADS * SSM_HEAD_DIM
SSM_GROUPS = 8
D_STATE = 128
CONV_K = 4
CHUNK = 128
C_CONV = D_SSM + 2 * SSM_GROUPS * D_STATE
ATT_HEADS = 16
ATT_HEAD_DIM = 128
D_ATT_QK = ATT_HEADS * 2 * ATT_HEAD_DIM
D_ATT_V = ATT_HEADS * 2 * ATT_HEAD_DIM
Q_BLOCK = 128
D_FF = 4 * D_MODEL
N_IN = 2 * D_SSM + 2 * SSM_GROUPS * D_STATE + SSM_HEADS + 2 * D_ATT_QK + D_ATT_V + 2 * D_MODEL
EPS = 1e-6
SUBLN_EPS = 1e-5
NEG_BIG = -1e30

kernel_name = "hybrid_ssd_diffattn_alibi_sandwich_meta"


def _split_points():
    sizes = [D_SSM, D_SSM, SSM_GROUPS * D_STATE, SSM_GROUPS * D_STATE, SSM_HEADS,
             D_ATT_QK, D_ATT_QK, D_ATT_V, D_MODEL, D_MODEL]
    return np.cumsum(sizes)[:-1].tolist()


def rmsnorm(x, w, eps=EPS):
    x32 = x.astype(jnp.float32)
    y = x32 * lax.rsqrt(jnp.mean(x32 * x32, axis=-1, keepdims=True) + eps)
    return y.astype(x.dtype) * w


def causal_depthwise_conv(u, w, b):
    c = u.shape[-1]
    out = lax.conv_general_dilated(u, w[:, None, :].astype(u.dtype), window_strides=(1,),
                                   padding=[(CONV_K - 1, 0)],
                                   dimension_numbers=('NWC', 'WIO', 'NWC'),
                                   feature_group_count=c)
    return out + b


def ssd_chunked(x, dt, a, bmat, cmat):
    f32 = jnp.float32
    b_, t_, h_, p_ = x.shape
    g_ = bmat.shape[2]
    n_ = bmat.shape[3]
    r_ = h_ // g_
    c_ = t_ // CHUNK
    xdt = (x.astype(f32) * dt[..., None]).reshape(b_, c_, CHUNK, g_, r_, p_)
    adt = jnp.moveaxis((dt * a).reshape(b_, c_, CHUNK, g_, r_), 2, -1)
    bm = bmat.astype(f32).reshape(b_, c_, CHUNK, g_, n_)
    cm = cmat.astype(f32).reshape(b_, c_, CHUNK, g_, n_)
    a_cum = jnp.cumsum(adt, axis=-1)
    causal = jnp.tril(jnp.ones((CHUNK, CHUNK), dtype=bool))
    seg = a_cum[..., :, None] - a_cum[..., None, :]
    decay = jnp.exp(jnp.where(causal, seg, -jnp.inf))
    cb = jnp.einsum('bclgn,bcsgn->bcgls', cm, bm)
    y_diag = jnp.einsum('bcgls,bcgrls,bcsgrp->bclgrp', cb, decay, xdt)
    decay_states = jnp.exp(a_cum[..., -1:] - a_cum)
    states = jnp.einsum('bclgn,bcgrl,bclgrp->bcgrpn', bm, decay_states, xdt)
    chunk_decay = jnp.exp(a_cum[..., -1])

    def step(hstate, inp):
        s, d = inp
        return hstate * d[..., None, None] + s, hstate

    h0 = jnp.zeros((b_, g_, r_, p_, n_), f32)
    _, prev = lax.scan(step, h0, (jnp.moveaxis(states, 1, 0), jnp.moveaxis(chunk_decay, 1, 0)))
    prev = jnp.moveaxis(prev, 0, 1)
    y_off = jnp.einsum('bclgn,bcgrpn,bcgrl->bclgrp', cm, prev, jnp.exp(a_cum))
    return (y_diag + y_off).reshape(b_, t_, h_, p_)


def mamba2_branch(z, xs, bs, cs, dt_raw, conv_w, conv_b, dt_bias, a_log, d_skip, norm_w, w_out):
    b_, l_, _ = xs.shape
    xbc = jax.nn.silu(causal_depthwise_conv(jnp.concatenate([xs, bs, cs], axis=-1), conv_w, conv_b))
    xs = xbc[..., :D_SSM]
    bs = xbc[..., D_SSM:D_SSM + SSM_GROUPS * D_STATE]
    cs = xbc[..., D_SSM + SSM_GROUPS * D_STATE:]
    dt = jax.nn.softplus(dt_raw.astype(jnp.float32) + dt_bias.astype(jnp.float32))
    a = -jnp.exp(a_log.astype(jnp.float32))
    pad = (-N_META) % CHUNK
    def lpad(t):
        return jnp.pad(t, ((0, 0), (pad, 0)) + ((0, 0),) * (t.ndim - 2))
    xh = xs.reshape(b_, l_, SSM_HEADS, SSM_HEAD_DIM)
    y = ssd_chunked(lpad(xh), lpad(dt), a,
                    lpad(bs.reshape(b_, l_, SSM_GROUPS, D_STATE)),
                    lpad(cs.reshape(b_, l_, SSM_GROUPS, D_STATE)))[:, pad:]
    y = (y + xh.astype(jnp.float32) * d_skip[:, None]).astype(xs.dtype).reshape(b_, l_, D_SSM)
    g = (y * jax.nn.silu(z)).reshape(b_, l_, SSM_GROUPS, D_SSM // SSM_GROUPS)
    g32 = g.astype(jnp.float32)
    g = (g32 * lax.rsqrt(jnp.mean(g32 * g32, axis=-1, keepdims=True) + EPS)).astype(y.dtype)
    g = g.reshape(b_, l_, D_SSM) * norm_w
    return g @ w_out


def diff_attention_branch(q, k, v, lam_q1, lam_k1, lam_q2, lam_k2, subln_w, w_out, layer_idx):
    f32 = jnp.float32
    b_, l_, _ = q.shape
    t_pad = -(-l_ // Q_BLOCK) * Q_BLOCK
    n_blk = t_pad // Q_BLOCK
    def rpad(t):
        return jnp.pad(t, ((0, 0), (0, t_pad - l_), (0, 0)))
    q = rpad(q).reshape(b_, t_pad, ATT_HEADS, 2, ATT_HEAD_DIM)
    k = rpad(k).reshape(b_, t_pad, ATT_HEADS, 2, ATT_HEAD_DIM)
    v = rpad(v).reshape(b_, t_pad, ATT_HEADS, 2 * ATT_HEAD_DIM)
    lambda_init = 0.8 - 0.6 * math.exp(-0.3 * layer_idx)
    lam = (jnp.exp(jnp.sum(lam_q1.astype(f32) * lam_k1.astype(f32)))
           - jnp.exp(jnp.sum(lam_q2.astype(f32) * lam_k2.astype(f32))) + lambda_init)
    slopes = 2.0 ** (-8.0 * jnp.arange(1, ATT_HEADS + 1, dtype=f32) / ATT_HEADS)
    scale = ATT_HEAD_DIM ** -0.5
    kpos = jnp.arange(t_pad)
    qb = jnp.moveaxis(q.reshape(b_, n_blk, Q_BLOCK, ATT_HEADS, 2, ATT_HEAD_DIM), 1, 0)

    def block(args):
        qblk, i = args
        qpos = i * Q_BLOCK + jnp.arange(Q_BLOCK)
        dist = (qpos[:, None] - kpos[None, :]).astype(f32)
        bias = -slopes[:, None, None] * dist
        s = jnp.einsum('bqhmd,bkhmd->bmhqk', qblk, k).astype(f32) * scale + bias
        s = jnp.where(dist >= 0, s, NEG_BIG)
        p = jax.nn.softmax(s, axis=-1)
        w = p[:, 0] - lam * p[:, 1]
        return jnp.einsum('bhqk,bkhe->bqhe', w.astype(v.dtype), v)

    o = lax.map(block, (qb, jnp.arange(n_blk)))
    o = jnp.moveaxis(o, 0, 1).reshape(b_, t_pad, ATT_HEADS, 2 * ATT_HEAD_DIM)[:, :l_]
    o = rmsnorm(o, subln_w, SUBLN_EPS) * (1.0 - lambda_init)
    return o.reshape(b_, l_, D_ATT_V) @ w_out


def setup_inputs(seed: int = 0) -> dict:
    key = jax.random.key(seed)
    ks = jax.random.split(key, 24)
    f32 = jnp.float32
    def nrm(k, shape, fan_in):
        return jax.random.normal(k, shape, f32) * (fan_in ** -0.5)
    def gain(k, shape):
        return 1.0 + 0.02 * jax.random.normal(k, shape, f32)
    dt0 = jnp.exp(jax.random.uniform(ks[6], (DEPTH, SSM_HEADS), f32)
                  * (math.log(0.1) - math.log(1e-3)) + math.log(1e-3))
    dt_bias = dt0 + jnp.log(-jnp.expm1(-dt0))
    a_log = jnp.log(jax.random.uniform(ks[7], (DEPTH, SSM_HEADS), f32, 1.0, 16.0))
    return {
        "x": jax.random.normal(ks[0], (BATCH, SEQ, D_MODEL), f32),
        "meta_tokens": jax.random.normal(ks[1], (N_META, D_MODEL), f32),
        "norm_pre_mix": gain(ks[2], (DEPTH, D_MODEL)),
        "w_in": nrm(ks[3], (DEPTH, D_MODEL, N_IN), D_MODEL),
        "conv_w": nrm(ks[4], (DEPTH, CONV_K, C_CONV), CONV_K),
        "conv_b": 0.01 * jax.random.normal(ks[5], (DEPTH, C_CONV), f32),
        "dt_bias": dt_bias,
        "a_log": a_log,
        "d_skip": gain(ks[8], (DEPTH, SSM_HEADS)),
        "ssm_norm": gain(ks[9], (DEPTH, D_SSM)),
        "w_ssm_out": nrm(ks[10], (DEPTH, D_SSM, D_MODEL), D_SSM),
        "lam_q1": 0.1 * jax.random.normal(ks[11], (DEPTH, ATT_HEAD_DIM), f32),
        "lam_k1": 0.1 * jax.random.normal(ks[12], (DEPTH, ATT_HEAD_DIM), f32),
        "lam_q2": 0.1 * jax.random.normal(ks[13], (DEPTH, ATT_HEAD_DIM), f32),
        "lam_k2": 0.1 * jax.random.normal(ks[14], (DEPTH, ATT_HEAD_DIM), f32),
        "subln": gain(ks[15], (DEPTH, 2 * ATT_HEAD_DIM)),
        "w_att_out": nrm(ks[16], (DEPTH, D_ATT_V, D_MODEL), D_ATT_V),
        "w_o": nrm(ks[17], (DEPTH, D_MODEL, D_MODEL), D_MODEL),
        "norm_post_mix": gain(ks[18], (DEPTH, D_MODEL)),
        "norm_pre_mlp": gain(ks[19], (DEPTH, D_MODEL)),
        "w_ff1": nrm(ks[20], (DEPTH, D_MODEL, D_FF), D_MODEL),
        "w_ff2": nrm(ks[21], (DEPTH, D_FF, D_MODEL), D_FF),
        "norm_post_mlp": gain(ks[22], (DEPTH, D_MODEL)),
    }


def reference(x, meta_tokens, norm_pre_mix, w_in, conv_w, conv_b, dt_bias, a_log, d_skip, ssm_norm,
              w_ssm_out, lam_q1, lam_k1, lam_q2, lam_k2, subln, w_att_out, w_o, norm_post_mix,
              norm_pre_mlp, w_ff1, w_ff2, norm_post_mlp):
    b_ = x.shape[0]
    meta = jnp.broadcast_to(meta_tokens[None].astype(x.dtype), (b_, N_META, D_MODEL))
    h = jnp.concatenate([meta, x], axis=1)
    splits = _split_points()
    for i in range(DEPTH):
        u = rmsnorm(h, norm_pre_mix[i])
        proj = u @ w_in[i]
        z, xs, bs, cs, dt_raw, q, k, v, g_ssm, g_att = jnp.split(proj, splits, axis=-1)
        y_ssm = mamba2_branch(z, xs, bs, cs, dt_raw, conv_w[i], conv_b[i], dt_bias[i], a_log[i],
                              d_skip[i], ssm_norm[i], w_ssm_out[i])
        y_att = diff_attention_branch(q, k, v, lam_q1[i], lam_k1[i], lam_q2[i], lam_k2[i],
                                      subln[i], w_att_out[i], i)
        mixed = jax.nn.sigmoid(g_ssm) * y_ssm + jax.nn.sigmoid(g_att) * y_att
        h = h + rmsnorm(mixed @ w_o[i], norm_post_mix[i])
        u = rmsnorm(h, norm_pre_mlp[i])
        f = jnp.square(jax.nn.relu(u @ w_ff1[i])) @ w_ff2[i]
        h = h + rmsnorm(f, norm_post_mlp[i])
    return h[:, N_META:]
```

```python
import functools
import math

import jax
import jax.numpy as jnp
from jax import lax
from jax.experimental import pallas as pl
from jax.experimental.pallas import tpu as pltpu

F32 = jnp.float32
BF16 = jnp.bfloat16

N_META = 16
ROW_BLOCK = 128
LEAD_PAD = ROW_BLOCK - N_META
SSM_HEAD_DIM = 64
SSM_GROUP_HEADS = 8
D_STATE = 128
CONV_K = 4
ATT_HEAD_DIM = 128
EPS = 1e-6
SUBLN_EPS = 1e-5
NEG_BIG = -1e30
LAMBDA_INIT = 0.8 - 0.6 * math.exp(-0.3 * 0)
HALO_ROWS = 16

VMEM_LIMIT_BYTES = 56 * 1024 * 1024


def _params(*sem):
    return pltpu.CompilerParams(dimension_semantics=sem, vmem_limit_bytes=VMEM_LIMIT_BYTES)


def _pick_tile(n, candidates):
    for c in candidates:
        if n % c == 0:
            return c
    return n


def _rms(x, eps):
    return x * lax.rsqrt(jnp.mean(x * x, axis=-1, keepdims=True) + eps)


def _rms_cast_kernel(h_ref, w_ref, o_ref):
    o_ref[...] = (_rms(h_ref[...], EPS) * w_ref[...]).astype(o_ref.dtype)


def rms_cast(h, w):
    l, d = h.shape
    tm = _pick_tile(l, (320, 256, 128))
    return pl.pallas_call(
        _rms_cast_kernel,
        grid=(l // tm,),
        in_specs=[pl.BlockSpec((tm, d), lambda i: (i, 0)), pl.BlockSpec((1, d), lambda i: (0, 0))],
        out_specs=pl.BlockSpec((tm, d), lambda i: (i, 0)),
        out_shape=jax.ShapeDtypeStruct((l, d), BF16),
        compiler_params=_params("parallel"),
        name="rms_cast",
    )(h, w.reshape(1, d))


def _post_mix_kernel(h_ref, t_ref, wpost_ref, wpre_ref, h1_ref, u_ref):
    h1 = h_ref[...] + _rms(t_ref[...], EPS) * wpost_ref[...]
    h1_ref[...] = h1
    u_ref[...] = (_rms(h1, EPS) * wpre_ref[...]).astype(u_ref.dtype)


def post_mix(h, t, w_post, w_pre):
    l, d = h.shape
    tm = _pick_tile(l, (320, 256, 128))
    row = pl.BlockSpec((tm, d), lambda i: (i, 0))
    vec = pl.BlockSpec((1, d), lambda i: (0, 0))
    return pl.pallas_call(
        _post_mix_kernel,
        grid=(l // tm,),
        in_specs=[row, row, vec, vec],
        out_specs=[row, row],
        out_shape=[jax.ShapeDtypeStruct((l, d), F32), jax.ShapeDtypeStruct((l, d), BF16)],
        compiler_params=_params("parallel"),
        name="post_mix",
    )(h, t, w_post.reshape(1, d), w_pre.reshape(1, d))


def _post_mlp_kernel(h_ref, f_ref, w_ref, o_ref):
    o_ref[...] = h_ref[...] + _rms(f_ref[...], EPS) * w_ref[...]


def post_mlp(h1, f, w, seq):
    l, d = h1.shape
    tm = ROW_BLOCK
    skip = (l - seq) // tm
    row_in = pl.BlockSpec((tm, d), lambda i: (i + skip, 0))
    return pl.pallas_call(
        _post_mlp_kernel,
        grid=(seq // tm,),
        in_specs=[row_in, row_in, pl.BlockSpec((1, d), lambda i: (0, 0))],
        out_specs=pl.BlockSpec((tm, d), lambda i: (i, 0)),
        out_shape=jax.ShapeDtypeStruct((seq, d), F32),
        compiler_params=_params("parallel"),
        name="post_mlp",
    )(h1, f, w.reshape(1, d))


def _mm_kernel(x_ref, w_ref, o_ref, *, epilogue):
    acc = jnp.dot(x_ref[...], w_ref[...], preferred_element_type=F32)
    o_ref[...] = epilogue(acc, pl.program_id(1)).astype(o_ref.dtype)


def _mm_tiles(m, n):
    tm = _pick_tile(m, (1664, 1280, 1024, 640, 512, 256, 128))
    tn = _pick_tile(n, (512, 256, 128))
    return tm, tn


def matmul(x, w, out_dtype, epilogue=lambda acc, j: acc, name="matmul"):
    m, k = x.shape
    n = w.shape[1]
    tm, tn = _mm_tiles(m, n)
    return pl.pallas_call(
        functools.partial(_mm_kernel, epilogue=epilogue),
        grid=(m // tm, n // tn),
        in_specs=[pl.BlockSpec((tm, k), lambda i, j: (i, 0)), pl.BlockSpec((k, tn), lambda i, j: (0, j))],
        out_specs=pl.BlockSpec((tm, tn), lambda i, j: (i, j)),
        out_shape=jax.ShapeDtypeStruct((m, n), out_dtype),
        compiler_params=_params("parallel", "arbitrary"),
        name=name,
    )(x, w)


def _mm_acc_kernel(x_ref, w_ref, o_ref, acc_ref):
    kk = pl.program_id(2)
    part = jnp.dot(x_ref[...], w_ref[...], preferred_element_type=F32)

    @pl.when(kk == 0)
    def _():
        acc_ref[...] = part

    @pl.when(kk != 0)
    def _():
        acc_ref[...] += part

    @pl.when(kk == pl.num_programs(2) - 1)
    def _():
        o_ref[...] = acc_ref[...].astype(o_ref.dtype)


def matmul_long_k(x, w, out_dtype, name="matmul_long_k"):
    m, k = x.shape
    n = w.shape[1]
    tm = _pick_tile(m, (1664, 1280, 1024, 640, 512, 256, 128))
    tn = _pick_tile(n, (1024, 512, 256, 128))
    tk = _pick_tile(k, (2048, 1024, 512))
    return pl.pallas_call(
        _mm_acc_kernel,
        grid=(m // tm, n // tn, k // tk),
        in_specs=[pl.BlockSpec((tm, tk), lambda i, j, kk: (i, kk)), pl.BlockSpec((tk, tn), lambda i, j, kk: (kk, j))],
        out_specs=pl.BlockSpec((tm, tn), lambda i, j, kk: (i, j)),
        out_shape=jax.ShapeDtypeStruct((m, n), out_dtype),
        scratch_shapes=[pltpu.VMEM((tm, tn), F32)],
        compiler_params=_params("parallel", "arbitrary", "arbitrary"),
        name=name,
    )(x, w)


def _gated_merge_kernel(ys_ref, ya_ref, ws_ref, wa_ref, gs_ref, ga_ref, o_ref):
    a = jnp.dot(ys_ref[...], ws_ref[...], preferred_element_type=F32)
    b = jnp.dot(ya_ref[...], wa_ref[...], preferred_element_type=F32)
    gs = jax.nn.sigmoid(gs_ref[...].astype(F32))
    ga = jax.nn.sigmoid(ga_ref[...].astype(F32))
    o_ref[...] = (gs * a + ga * b).astype(o_ref.dtype)


def gated_merge(y_ssm, y_att, w_ssm, w_att, proj_b, gate_col0):
    m, k = y_ssm.shape
    n = w_ssm.shape[1]
    tm = _pick_tile(m, (832, 640, 512, 256, 128))
    tn = _pick_tile(n, (512, 256, 128))
    gs_blk = gate_col0 // tn
    ga_blk = (gate_col0 + n) // tn
    ka = y_att.shape[1]
    return pl.pallas_call(
        _gated_merge_kernel,
        grid=(m // tm, n // tn),
        in_specs=[pl.BlockSpec((tm, k), lambda i, j: (i, 0)), pl.BlockSpec((tm, ka), lambda i, j: (i, 0)),
                  pl.BlockSpec((k, tn), lambda i, j: (0, j)), pl.BlockSpec((ka, tn), lambda i, j: (0, j)),
                  pl.BlockSpec((tm, tn), lambda i, j: (i, gs_blk + j)),
                  pl.BlockSpec((tm, tn), lambda i, j: (i, ga_blk + j))],
        out_specs=pl.BlockSpec((tm, tn), lambda i, j: (i, j)),
        out_shape=jax.ShapeDtypeStruct((m, n), BF16),
        compiler_params=_params("parallel", "arbitrary"),
        name="gated_merge",
    )(y_ssm, y_att, w_ssm, w_att, proj_b, proj_b)


def _split_bf16(x, parts):
    out = []
    r = x
    for _ in range(parts):
        p = r.astype(BF16)
        out.append(p)
        r = r - p.astype(F32)
    return out


def _dot_onehot(x, onehot, parts, dims=(((1,), (0,)), ((), ()))):
    acc = None
    for p in _split_bf16(x, parts):
        t = lax.dot_general(p, onehot, dims, preferred_element_type=F32)
        acc = t if acc is None else acc + t
    return acc


def _causal_conv_silu(cur_ref, halo_ref, w_ref, b_ref, first_chunk):
    cur = cur_ref[...].astype(F32)
    halo = halo_ref[...].astype(F32)[HALO_ROWS - 8:, :]
    halo = jnp.where(first_chunk, 0.0, halo)
    row8 = lax.broadcasted_iota(jnp.int32, (8, 1), 0)
    acc = b_ref[...] + w_ref[CONV_K - 1:CONV_K, :] * cur
    for shift in range(1, CONV_K):
        rolled = pltpu.roll(cur, shift, 0)
        top = jnp.where(row8 < shift, pltpu.roll(halo, shift, 0), rolled[0:8, :])
        shifted = jnp.concatenate([top, rolled[8:, :]], axis=0)
        acc = acc + w_ref[CONV_K - 1 - shift:CONV_K - shift, :] * shifted
    return acc * jax.nn.sigmoid(acc)


def _ssd_kernel(z_ref, xs_ref, b_ref, c_ref, xsh_ref, bh_ref, ch_ref, dt_ref,
                cwx_ref, cbx_ref, cwb_ref, cbb_ref, cwc_ref, cbc_ref,
                dtb_ref, alog_ref, dskip_ref, nw_ref, o_ref, state_ref):
    g = pl.program_id(0)
    c = pl.program_id(1)
    L = ROW_BLOCK
    gw = SSM_GROUP_HEADS * SSM_HEAD_DIM
    first = c == 0

    @pl.when(first)
    def _():
        state_ref[...] = jnp.zeros_like(state_ref)

    xs = _causal_conv_silu(xs_ref, xsh_ref, cwx_ref, cbx_ref, first)
    bm = _causal_conv_silu(b_ref, bh_ref, cwb_ref, cbb_ref, first)
    cm = _causal_conv_silu(c_ref, ch_ref, cwc_ref, cbc_ref, first)

    dt = jax.nn.softplus(dt_ref[...] + dtb_ref[...])
    row = c * L + lax.broadcasted_iota(jnp.int32, (L, 1), 0)
    dt = jnp.where(row >= LEAD_PAD, dt, 0.0)
    adt = dt * (-jnp.exp(alog_ref[...]))
    r_i = lax.broadcasted_iota(jnp.int32, (L, L), 0)
    c_i = lax.broadcasted_iota(jnp.int32, (L, L), 1)
    causal = r_i >= c_i
    tri = causal.astype(BF16)
    adt_parts = _split_bf16(adt, 3)
    a_cum = sum(jnp.dot(tri, p, preferred_element_type=F32) for p in adt_parts)
    pick = (g * SSM_GROUP_HEADS + lax.broadcasted_iota(jnp.int32, (HALO_ROWS, L), 0)
            == lax.broadcasted_iota(jnp.int32, (HALO_ROWS, L), 1)).astype(BF16)
    a_t_grp = sum(lax.dot_general(pick, p, (((1,), (1,)), ((), ())), preferred_element_type=F32)
                  for p in _split_bf16(a_cum, 3))

    hl = lax.broadcasted_iota(jnp.int32, (L, gw), 0)
    ch_head = g * SSM_GROUP_HEADS + lax.broadcasted_iota(jnp.int32, (L, gw), 1) // SSM_HEAD_DIM
    spread = (hl == ch_head).astype(BF16)
    hl2 = lax.broadcasted_iota(jnp.int32, (L, SSM_GROUP_HEADS * L), 0)
    col_head = g * SSM_GROUP_HEADS + lax.broadcasted_iota(jnp.int32, (L, SSM_GROUP_HEADS * L), 1) // L
    spread_wide = (hl2 == col_head).astype(BF16)

    a_ch = _dot_onehot(a_cum, spread, 3)
    dt_ch = _dot_onehot(dt, spread, 3)
    a_col = _dot_onehot(a_cum, spread_wide, 3)

    xdt = xs * dt_ch
    a_last = a_ch[L - 1:L, :]
    cm_b = cm.astype(BF16)
    bm_b = bm.astype(BF16)
    cb = lax.dot_general(cm_b, bm_b, (((1,), (1,)), ((), ())), preferred_element_type=F32)

    state = state_ref[...]
    y = jnp.dot(cm_b, state.astype(BF16), preferred_element_type=F32) * jnp.exp(a_ch)
    decayed = (xdt * jnp.exp(a_last - a_ch)).astype(BF16)
    state_ref[...] = state * jnp.exp(a_last) + jnp.dot(bm.T.astype(BF16), decayed, preferred_element_type=F32)

    xdt_b = xdt.astype(BF16)
    lane_lo = lax.broadcasted_iota(jnp.int32, (1, 2 * SSM_HEAD_DIM), 1) < SSM_HEAD_DIM
    y_pairs = []
    for pp in range(SSM_GROUP_HEADS // 2):
        x_pair = xdt_b[:, pp * 2 * SSM_HEAD_DIM:(pp + 1) * 2 * SSM_HEAD_DIM]
        y_pair = None
        for half in range(2):
            hh = 2 * pp + half
            seg = a_col[:, hh * L:(hh + 1) * L] - a_t_grp[hh:hh + 1, :]
            decay = jnp.exp(jnp.where(causal, seg, -jnp.inf))
            lhs = (cb * decay).astype(BF16)
            rhs = jnp.where(lane_lo if half == 0 else jnp.logical_not(lane_lo), x_pair, jnp.zeros_like(x_pair))
            t = jnp.dot(lhs, rhs, preferred_element_type=F32)
            y_pair = t if y_pair is None else y_pair + t
        y_pairs.append(y_pair)
    y = y + jnp.concatenate(y_pairs, axis=1) + xs * dskip_ref[...]

    z = z_ref[...].astype(F32)
    gt = y * (z * jax.nn.sigmoid(z))
    o_ref[...] = (_rms(gt, EPS) * nw_ref[...]).astype(o_ref.dtype)


def ssd_branch(proj_a, dt_raw, conv_w, conv_b, dt_bias, a_log, d_skip, ssm_norm, d_ssm, n_groups):
    l = proj_a.shape[0]
    L = ROW_BLOCK
    gw = SSM_GROUP_HEADS * SSM_HEAD_DIM
    n = D_STATE
    n_heads = d_ssm // SSM_HEAD_DIM
    pad_h = L - n_heads
    halo_per_chunk = L // HALO_ROWS

    def halo_idx(c):
        return jnp.maximum(c * halo_per_chunk - 1, 0)

    x_col0 = d_ssm // gw
    b_col0 = 2 * d_ssm // n
    c_col0 = b_col0 + n_groups
    cw_b0 = d_ssm // n
    cw_c0 = cw_b0 + n_groups
    vec128 = pl.BlockSpec((1, L), lambda g, c: (0, 0))

    def padv(v):
        return jnp.pad(v.astype(F32), (0, pad_h)).reshape(1, L)

    return pl.pallas_call(
        _ssd_kernel,
        grid=(n_groups, l // L),
        in_specs=[
            pl.BlockSpec((L, gw), lambda g, c: (c, g)),
            pl.BlockSpec((L, gw), lambda g, c: (c, x_col0 + g)),
            pl.BlockSpec((L, n), lambda g, c: (c, b_col0 + g)),
            pl.BlockSpec((L, n), lambda g, c: (c, c_col0 + g)),
            pl.BlockSpec((HALO_ROWS, gw), lambda g, c: (halo_idx(c), x_col0 + g)),
            pl.BlockSpec((HALO_ROWS, n), lambda g, c: (halo_idx(c), b_col0 + g)),
            pl.BlockSpec((HALO_ROWS, n), lambda g, c: (halo_idx(c), c_col0 + g)),
            pl.BlockSpec((L, L), lambda g, c: (c, 0)),
            pl.BlockSpec((CONV_K, gw), lambda g, c: (0, g)),
            pl.BlockSpec((1, gw), lambda g, c: (0, g)),
            pl.BlockSpec((CONV_K, n), lambda g, c: (0, cw_b0 + g)),
            pl.BlockSpec((1, n), lambda g, c: (0, cw_b0 + g)),
            pl.BlockSpec((CONV_K, n), lambda g, c: (0, cw_c0 + g)),
            pl.BlockSpec((1, n), lambda g, c: (0, cw_c0 + g)),
            vec128, vec128,
            pl.BlockSpec((1, gw), lambda g, c: (0, g)),
            pl.BlockSpec((1, gw), lambda g, c: (0, g)),
        ],
        out_specs=pl.BlockSpec((L, gw), lambda g, c: (c, g)),
        out_shape=jax.ShapeDtypeStruct((l, d_ssm), BF16),
        scratch_shapes=[pltpu.VMEM((n, gw), F32)],
        compiler_params=_params("parallel", "arbitrary"),
        name="ssd",
    )(proj_a, proj_a, proj_a, proj_a, proj_a, proj_a, proj_a, dt_raw,
      conv_w, conv_b.reshape(1, -1), conv_w, conv_b.reshape(1, -1), conv_w, conv_b.reshape(1, -1),
      padv(dt_bias), padv(a_log),
      jnp.repeat(d_skip.astype(F32), SSM_HEAD_DIM).reshape(1, d_ssm), ssm_norm.reshape(1, d_ssm))


def _attn_kernel(q_ref, k_ref, v_ref, slope_ref, lq1_ref, lk1_ref, lq2_ref, lk2_ref, subln_ref,
                 o_ref, m_ref, l_ref, acc_ref, *, tq, tk):
    qi = pl.program_id(1)
    d = ATT_HEAD_DIM
    q0 = qi * tq
    slope = slope_ref[0, 0:1, 0:1]
    qpos = q0 + lax.broadcasted_iota(jnp.int32, (tq, 1), 0)

    m_ref[...] = jnp.full(m_ref.shape, NEG_BIG, F32)
    l_ref[...] = jnp.zeros(l_ref.shape, F32)
    acc_ref[...] = jnp.zeros(acc_ref.shape, F32)

    def step(jk, masked):
        ks = pl.multiple_of(jk * tk, tk)
        vblk = v_ref[pl.ds(ks, tk), :]
        kpos = ks + lax.broadcasted_iota(jnp.int32, (1, tk), 1)
        bias = jnp.where(kpos >= LEAD_PAD, slope * (kpos - q0).astype(F32), NEG_BIG)
        for i in range(2):
            s = lax.dot_general(q_ref[:, i * d:(i + 1) * d], k_ref[pl.ds(ks, tk), i * d:(i + 1) * d],
                                (((1,), (1,)), ((), ())), preferred_element_type=F32)
            s = s + bias
            if masked:
                s = jnp.where(kpos <= qpos, s, NEG_BIG)
            m_old = m_ref[i]
            m_new = jnp.maximum(m_old, jnp.max(s, axis=-1, keepdims=True))
            alpha = jnp.exp(m_old - m_new)
            p = jnp.exp(s - m_new)
            l_ref[i] = alpha * l_ref[i] + jnp.sum(p, axis=-1, keepdims=True)
            acc_ref[i] = alpha * acc_ref[i] + jnp.dot(p.astype(BF16), vblk, preferred_element_type=F32)
            m_ref[i] = m_new

    n_full = (qi * tq) // tk

    def body(jk, carry):
        step(jk, False)
        return carry

    lax.fori_loop(0, n_full, body, 0)
    for jd in range(tq // tk):
        step(n_full + jd, True)

    lam = (jnp.exp(jnp.sum(lq1_ref[...] * lk1_ref[...], axis=-1, keepdims=True))
           - jnp.exp(jnp.sum(lq2_ref[...] * lk2_ref[...], axis=-1, keepdims=True)) + LAMBDA_INIT)
    o = acc_ref[0] / l_ref[0] - lam * (acc_ref[1] / l_ref[1])
    o = _rms(o, SUBLN_EPS) * subln_ref[...] * (1.0 - LAMBDA_INIT)
    o_ref[...] = o.astype(o_ref.dtype)


def diff_attention(proj_b, n_heads, lam_q1, lam_k1, lam_q2, lam_k2, subln):
    l = proj_b.shape[0]
    hd = 2 * ATT_HEAD_DIM
    tq = _pick_tile(l, (640, 512, 256, 128))
    tk = tq
    slopes = 2.0 ** (-8.0 * jnp.arange(1, n_heads + 1, dtype=F32) / n_heads)
    slopes = jnp.broadcast_to(slopes[:, None, None], (n_heads, 8, 128))
    vec = pl.BlockSpec((1, ATT_HEAD_DIM), lambda h, i: (0, 0))
    return pl.pallas_call(
        functools.partial(_attn_kernel, tq=tq, tk=tk),
        grid=(n_heads, l // tq),
        in_specs=[
            pl.BlockSpec((tq, hd), lambda h, i: (i, h)),
            pl.BlockSpec((l, hd), lambda h, i: (0, n_heads + h)),
            pl.BlockSpec((l, hd), lambda h, i: (0, 2 * n_heads + h)),
            pl.BlockSpec((1, 8, 128), lambda h, i: (h, 0, 0)),
            vec, vec, vec, vec,
            pl.BlockSpec((1, hd), lambda h, i: (0, 0)),
        ],
        out_specs=pl.BlockSpec((tq, hd), lambda h, i: (i, h)),
        out_shape=jax.ShapeDtypeStruct((l, n_heads * hd), BF16),
        scratch_shapes=[pltpu.VMEM((2, tq, 1), F32), pltpu.VMEM((2, tq, 1), F32), pltpu.VMEM((2, tq, hd), F32)],
        compiler_params=_params("parallel", "arbitrary"),
        name="diff_attention",
    )(proj_b, proj_b, proj_b, slopes,
      lam_q1.reshape(1, -1), lam_k1.reshape(1, -1), lam_q2.reshape(1, -1), lam_k2.reshape(1, -1),
      subln.reshape(1, hd))


def kernel(x, meta_tokens, norm_pre_mix, w_in, conv_w, conv_b, dt_bias, a_log, d_skip, ssm_norm, w_ssm_out,
           lam_q1, lam_k1, lam_q2, lam_k2, subln, w_att_out, w_o, norm_post_mix, norm_pre_mlp, w_ff1, w_ff2,
           norm_post_mlp):
    batch, seq, d_model = x.shape
    assert batch == 1 and w_in.shape[0] == 1 and seq % ROW_BLOCK == 0
    d_ssm = w_ssm_out.shape[1]
    d_att = w_att_out.shape[1]
    n_ssm_heads = d_ssm // SSM_HEAD_DIM
    n_groups = n_ssm_heads // SSM_GROUP_HEADS
    n_att_heads = d_att // (2 * ATT_HEAD_DIM)
    assert n_ssm_heads <= ROW_BLOCK
    n_a = 2 * d_ssm + 2 * n_groups * D_STATE
    n_dt = n_ssm_heads
    assert w_in.shape[2] == n_a + n_dt + 3 * d_att + 2 * d_model

    h = jnp.concatenate([jnp.zeros((LEAD_PAD, d_model), F32), meta_tokens.astype(F32), x[0]], axis=0)

    w_a = w_in[0, :, :n_a].astype(BF16)
    w_dt = jnp.pad(w_in[0, :, n_a:n_a + n_dt], ((0, 0), (0, ROW_BLOCK - n_dt))).astype(BF16)
    w_b = w_in[0, :, n_a + n_dt:].astype(BF16)

    u = rms_cast(h, norm_pre_mix[0])
    proj_a = matmul(u, w_a, BF16, name="in_proj_ssm")
    dt_raw = matmul(u, w_dt, F32, name="in_proj_dt")
    scale = ATT_HEAD_DIM ** -0.5
    q_tiles = d_att // _mm_tiles(h.shape[0], w_b.shape[1])[1]
    proj_b = matmul(u, w_b, BF16, name="in_proj_att",
                    epilogue=lambda acc, j: acc * jnp.where(j < q_tiles, scale, 1.0))

    y_ssm = ssd_branch(proj_a, dt_raw, conv_w[0], conv_b[0], dt_bias[0], a_log[0], d_skip[0], ssm_norm[0],
                       d_ssm, n_groups)
    y_att = diff_attention(proj_b, n_att_heads, lam_q1[0], lam_k1[0], lam_q2[0], lam_k2[0], subln[0])

    mixed = gated_merge(y_ssm, y_att, w_ssm_out[0].astype(BF16), w_att_out[0].astype(BF16), proj_b, 3 * d_att)
    t = matmul(mixed, w_o[0].astype(BF16), F32, name="out_proj")
    h1, u2 = post_mix(h, t, norm_post_mix[0], norm_pre_mlp[0])

    f1 = matmul(u2, w_ff1[0].astype(BF16), BF16, epilogue=lambda acc, j: jnp.square(jnp.maximum(acc, 0.0)),
                name="ff1")
    f2 = matmul_long_k(f1, w_ff2[0].astype(BF16), F32, name="ff2")
    out = post_mlp(h1, f2, norm_post_mlp[0], seq)
    return out[None]
```

```python
import functools
import math

import jax
import jax.numpy as jnp
from jax import lax
from jax.experimental import pallas as pl
from jax.experimental.pallas import tpu as pltpu

F32 = jnp.float32
BF16 = jnp.bfloat16

N_META = 16
ROW_BLOCK = 128
LEAD_PAD = ROW_BLOCK - N_META
SSM_HEAD_DIM = 64
SSM_GROUP_HEADS = 8
D_STATE = 128
CONV_K = 4
ATT_HEAD_DIM = 128
EPS = 1e-6
SUBLN_EPS = 1e-5
NEG_BIG = -1e30
LAMBDA_INIT = 0.8 - 0.6 * math.exp(-0.3 * 0)
HALO_ROWS = 16

VMEM_LIMIT_BYTES = 56 * 1024 * 1024


def _params(*sem):
    return pltpu.CompilerParams(dimension_semantics=sem, vmem_limit_bytes=VMEM_LIMIT_BYTES)


def _pick_tile(n, candidates):
    for c in candidates:
        if n % c == 0:
            return c
    return n


M_TILES = (1664, 1408, 1280, 1024, 640, 512, 256, 128)


def _rms(x, eps):
    return x * lax.rsqrt(jnp.mean(x * x, axis=-1, keepdims=True) + eps)


def _h_block(i, x_ref, meta_ref):
    lead = jnp.concatenate([jnp.zeros((LEAD_PAD, meta_ref.shape[1]), F32), meta_ref[...]], axis=0)
    return jnp.where(i == 0, lead, x_ref[...])


def _h_specs(d, n_tok_blocks):
    return [pl.BlockSpec((ROW_BLOCK, d), lambda i: (jnp.clip(i - 1, 0, n_tok_blocks - 1), 0)),
            pl.BlockSpec((N_META, d), lambda i: (0, 0))]


def _rms_cast_kernel(x_ref, meta_ref, w_ref, o_ref, *, n_tok_blocks):
    i = pl.program_id(0)
    h = jnp.where(i > n_tok_blocks, 0.0, _h_block(i, x_ref, meta_ref))
    o_ref[...] = (_rms(h, EPS) * w_ref[...]).astype(o_ref.dtype)


def rms_cast(x, meta, w, rows):
    seq, d = x.shape
    n_tok_blocks = seq // ROW_BLOCK
    return pl.pallas_call(
        functools.partial(_rms_cast_kernel, n_tok_blocks=n_tok_blocks),
        grid=(rows // ROW_BLOCK,),
        in_specs=_h_specs(d, n_tok_blocks) + [pl.BlockSpec((1, d), lambda i: (0, 0))],
        out_specs=pl.BlockSpec((ROW_BLOCK, d), lambda i: (i, 0)),
        out_shape=jax.ShapeDtypeStruct((rows, d), BF16),
        compiler_params=_params("parallel"),
        name="rms_cast",
    )(x, meta, w.reshape(1, d))


def _post_mix_kernel(x_ref, meta_ref, t_ref, wpost_ref, wpre_ref, h1_ref, u_ref):
    h1 = _h_block(pl.program_id(0), x_ref, meta_ref) + _rms(t_ref[...], EPS) * wpost_ref[...]
    h1_ref[...] = h1
    u_ref[...] = (_rms(h1, EPS) * wpre_ref[...]).astype(u_ref.dtype)


def post_mix(x, meta, t, w_post, w_pre):
    seq, d = x.shape
    l = t.shape[0]
    row = pl.BlockSpec((ROW_BLOCK, d), lambda i: (i, 0))
    vec = pl.BlockSpec((1, d), lambda i: (0, 0))
    return pl.pallas_call(
        _post_mix_kernel,
        grid=(l // ROW_BLOCK,),
        in_specs=_h_specs(d, seq // ROW_BLOCK) + [row, vec, vec],
        out_specs=[row, row],
        out_shape=[jax.ShapeDtypeStruct((l, d), F32), jax.ShapeDtypeStruct((l, d), BF16)],
        compiler_params=_params("parallel"),
        name="post_mix",
    )(x, meta, t, w_post.reshape(1, d), w_pre.reshape(1, d))


def _post_mlp_kernel(h_ref, f_ref, w_ref, o_ref):
    o_ref[...] = h_ref[...] + _rms(f_ref[...], EPS) * w_ref[...]


def post_mlp(h1, f, w, seq):
    l, d = h1.shape
    tm = ROW_BLOCK
    skip = (l - seq) // tm
    row_in = pl.BlockSpec((tm, d), lambda i: (i + skip, 0))
    return pl.pallas_call(
        _post_mlp_kernel,
        grid=(seq // tm,),
        in_specs=[row_in, row_in, pl.BlockSpec((1, d), lambda i: (0, 0))],
        out_specs=pl.BlockSpec((tm, d), lambda i: (i, 0)),
        out_shape=jax.ShapeDtypeStruct((seq, d), F32),
        compiler_params=_params("parallel"),
        name="post_mlp",
    )(h1, f, w.reshape(1, d))


def _mm_kernel(x_ref, w_ref, o_ref, *, epilogue):
    acc = jnp.dot(x_ref[...], w_ref[...].astype(BF16), preferred_element_type=F32)
    o_ref[...] = epilogue(acc, pl.program_id(1)).astype(o_ref.dtype)


def matmul(x, w, out_dtype, *, m=None, col0=0, n=None, tn_max=512, epilogue=lambda acc, j: acc, name="matmul"):
    k = x.shape[1]
    m = x.shape[0] if m is None else m
    n = w.shape[1] - col0 if n is None else n
    tm = _pick_tile(m, M_TILES)
    tn = _pick_tile(math.gcd(n, col0) if col0 else n, tuple(t for t in (512, 256, 128) if t <= tn_max))
    cb0 = col0 // tn
    return pl.pallas_call(
        functools.partial(_mm_kernel, epilogue=epilogue),
        grid=(m // tm, n // tn),
        in_specs=[pl.BlockSpec((tm, k), lambda i, j: (i, 0), pipeline_mode=pl.Buffered(1)),
                  pl.BlockSpec((k, tn), lambda i, j: (0, cb0 + j))],
        out_specs=pl.BlockSpec((tm, tn), lambda i, j: (i, j)),
        out_shape=jax.ShapeDtypeStruct((m, n), out_dtype),
        compiler_params=_params("parallel", "arbitrary"),
        name=name,
    )(x, w)


def _mm_acc_kernel(x_ref, w_ref, o_ref):
    @pl.when(pl.program_id(2) == 0)
    def _():
        o_ref[...] = jnp.zeros(o_ref.shape, F32)

    o_ref[...] += jnp.dot(x_ref[...], w_ref[...].astype(BF16), preferred_element_type=F32)


def matmul_long_k(x, w, name="matmul_long_k"):
    m, k = x.shape
    n = w.shape[1]
    tm = _pick_tile(m, M_TILES)
    tn = _pick_tile(n, (1024, 512, 256, 128))
    tk = _pick_tile(k, (2048, 1024, 512))
    return pl.pallas_call(
        _mm_acc_kernel,
        grid=(m // tm, n // tn, k // tk),
        in_specs=[pl.BlockSpec((tm, tk), lambda i, j, kk: (i, kk)), pl.BlockSpec((tk, tn), lambda i, j, kk: (kk, j))],
        out_specs=pl.BlockSpec((tm, tn), lambda i, j, kk: (i, j)),
        out_shape=jax.ShapeDtypeStruct((m, n), F32),
        compiler_params=_params("parallel", "arbitrary", "arbitrary"),
        name=name,
    )(x, w)


def _gated_merge_kernel(ys_ref, ya_ref, ws_ref, wa_ref, gs_ref, ga_ref, o_ref):
    a = jnp.dot(ys_ref[...], ws_ref[...].astype(BF16), preferred_element_type=F32)
    b = jnp.dot(ya_ref[...], wa_ref[...].astype(BF16), preferred_element_type=F32)
    gs = jax.nn.sigmoid(gs_ref[...].astype(F32))
    ga = jax.nn.sigmoid(ga_ref[...].astype(F32))
    o_ref[...] = (gs * a + ga * b).astype(o_ref.dtype)


def gated_merge(y_ssm, y_att, w_ssm, w_att, proj_b, gate_col0):
    m, k = y_ssm.shape
    ka = y_att.shape[1]
    n = w_ssm.shape[1]
    tm = _pick_tile(m, (832, 640, 512, 256, 128))
    tn = _pick_tile(n, (256, 128))
    gs_blk = gate_col0 // tn
    ga_blk = (gate_col0 + n) // tn
    return pl.pallas_call(
        _gated_merge_kernel,
        grid=(m // tm, n // tn),
        in_specs=[pl.BlockSpec((tm, k), lambda i, j: (i, 0), pipeline_mode=pl.Buffered(1)),
                  pl.BlockSpec((tm, ka), lambda i, j: (i, 0), pipeline_mode=pl.Buffered(1)),
                  pl.BlockSpec((k, tn), lambda i, j: (0, j)), pl.BlockSpec((ka, tn), lambda i, j: (0, j)),
                  pl.BlockSpec((tm, tn), lambda i, j: (i, gs_blk + j)),
                  pl.BlockSpec((tm, tn), lambda i, j: (i, ga_blk + j))],
        out_specs=pl.BlockSpec((tm, tn), lambda i, j: (i, j)),
        out_shape=jax.ShapeDtypeStruct((m, n), BF16),
        compiler_params=_params("parallel", "arbitrary"),
        name="gated_merge",
    )(y_ssm, y_att, w_ssm, w_att, proj_b, proj_b)


def _split_bf16(x, parts):
    out = []
    r = x
    for _ in range(parts):
        p = r.astype(BF16)
        out.append(p)
        r = r - p.astype(F32)
    return out


def _dot_onehot(x, onehot, parts, dims=(((1,), (0,)), ((), ()))):
    acc = None
    for p in _split_bf16(x, parts):
        t = lax.dot_general(p, onehot, dims, preferred_element_type=F32)
        acc = t if acc is None else acc + t
    return acc


def _causal_conv_silu(cur_ref, halo_ref, w_ref, b_ref, first_chunk):
    cur = cur_ref[...].astype(F32)
    halo = halo_ref[...].astype(F32)[HALO_ROWS - 8:, :]
    halo = jnp.where(first_chunk, 0.0, halo)
    row8 = lax.broadcasted_iota(jnp.int32, (8, 1), 0)
    acc = b_ref[...] + w_ref[CONV_K - 1:CONV_K, :] * cur
    for shift in range(1, CONV_K):
        rolled = pltpu.roll(cur, shift, 0)
        top = jnp.where(row8 < shift, pltpu.roll(halo, shift, 0), rolled[0:8, :])
        shifted = jnp.concatenate([top, rolled[8:, :]], axis=0)
        acc = acc + w_ref[CONV_K - 1 - shift:CONV_K - shift, :] * shifted
    return acc * jax.nn.sigmoid(acc)


def _ssd_kernel(z_ref, xs_ref, b_ref, c_ref, xsh_ref, bh_ref, ch_ref, dt_ref,
                cwx_ref, cbx_ref, cwb_ref, cbb_ref, cwc_ref, cbc_ref,
                dtb_ref, alog_ref, dskip_ref, nw_ref, o_ref, state_ref, *, n_heads):
    g = pl.program_id(0)
    c = pl.program_id(1)
    L = ROW_BLOCK
    gw = SSM_GROUP_HEADS * SSM_HEAD_DIM
    first = c == 0

    @pl.when(first)
    def _():
        state_ref[...] = jnp.zeros_like(state_ref)

    xs = _causal_conv_silu(xs_ref, xsh_ref, cwx_ref, cbx_ref, first)
    bm = _causal_conv_silu(b_ref, bh_ref, cwb_ref, cbb_ref, first)
    cm = _causal_conv_silu(c_ref, ch_ref, cwc_ref, cbc_ref, first)

    dt = jax.nn.softplus(dt_ref[...] + dtb_ref[...])
    row = c * L + lax.broadcasted_iota(jnp.int32, (L, 1), 0)
    lane = lax.broadcasted_iota(jnp.int32, (1, L), 1)
    dt = jnp.where((row >= LEAD_PAD) & (lane < n_heads), dt, 0.0)
    adt = dt * (-jnp.exp(alog_ref[...]))
    r_i = lax.broadcasted_iota(jnp.int32, (L, L), 0)
    c_i = lax.broadcasted_iota(jnp.int32, (L, L), 1)
    causal = r_i >= c_i
    tri = causal.astype(BF16)
    adt_parts = _split_bf16(adt, 3)
    a_cum = sum(jnp.dot(tri, p, preferred_element_type=F32) for p in adt_parts)
    pick = (g * SSM_GROUP_HEADS + lax.broadcasted_iota(jnp.int32, (HALO_ROWS, L), 0)
            == lax.broadcasted_iota(jnp.int32, (HALO_ROWS, L), 1)).astype(BF16)
    a_t_grp = sum(lax.dot_general(pick, p, (((1,), (1,)), ((), ())), preferred_element_type=F32)
                  for p in _split_bf16(a_cum, 3))

    hl = lax.broadcasted_iota(jnp.int32, (L, gw), 0)
    ch_head = g * SSM_GROUP_HEADS + lax.broadcasted_iota(jnp.int32, (L, gw), 1) // SSM_HEAD_DIM
    spread = (hl == ch_head).astype(BF16)
    hl2 = lax.broadcasted_iota(jnp.int32, (L, SSM_GROUP_HEADS * L), 0)
    col_head = g * SSM_GROUP_HEADS + lax.broadcasted_iota(jnp.int32, (L, SSM_GROUP_HEADS * L), 1) // L
    spread_wide = (hl2 == col_head).astype(BF16)

    a_ch = _dot_onehot(a_cum, spread, 3)
    dt_ch = _dot_onehot(dt, spread, 3)
    a_col = _dot_onehot(a_cum, spread_wide, 3)

    xdt = xs * dt_ch
    a_last = a_ch[L - 1:L, :]
    cm_b = cm.astype(BF16)
    bm_b = bm.astype(BF16)
    cb = lax.dot_general(cm_b, bm_b, (((1,), (1,)), ((), ())), preferred_element_type=F32)

    state = state_ref[...]
    y = jnp.dot(cm_b, state.astype(BF16), preferred_element_type=F32) * jnp.exp(a_ch)
    decayed = (xdt * jnp.exp(a_last - a_ch)).astype(BF16)
    state_ref[...] = state * jnp.exp(a_last) + jnp.dot(bm.T.astype(BF16), decayed, preferred_element_type=F32)

    xdt_b = xdt.astype(BF16)
    lane_lo = lax.broadcasted_iota(jnp.int32, (1, 2 * SSM_HEAD_DIM), 1) < SSM_HEAD_DIM
    y_pairs = []
    for pp in range(SSM_GROUP_HEADS // 2):
        x_pair = xdt_b[:, pp * 2 * SSM_HEAD_DIM:(pp + 1) * 2 * SSM_HEAD_DIM]
        y_pair = None
        for half in range(2):
            hh = 2 * pp + half
            seg = a_col[:, hh * L:(hh + 1) * L] - a_t_grp[hh:hh + 1, :]
            decay = jnp.exp(jnp.where(causal, seg, -jnp.inf))
            lhs = (cb * decay).astype(BF16)
            rhs = jnp.where(lane_lo if half == 0 else jnp.logical_not(lane_lo), x_pair, jnp.zeros_like(x_pair))
            t = jnp.dot(lhs, rhs, preferred_element_type=F32)
            y_pair = t if y_pair is None else y_pair + t
        y_pairs.append(y_pair)
    y = y + jnp.concatenate(y_pairs, axis=1) + xs * dskip_ref[...]

    z = z_ref[...].astype(F32)
    gt = y * (z * jax.nn.sigmoid(z))
    o_ref[...] = (_rms(gt, EPS) * nw_ref[...]).astype(o_ref.dtype)


def ssd_branch(proj_a, dt_raw, conv_w, conv_b, dt_bias, a_log, d_skip, ssm_norm, d_ssm, n_groups):
    l = proj_a.shape[0]
    L = ROW_BLOCK
    gw = SSM_GROUP_HEADS * SSM_HEAD_DIM
    n = D_STATE
    n_heads = d_ssm // SSM_HEAD_DIM
    pad_h = L - n_heads
    halo_per_chunk = L // HALO_ROWS

    def halo_idx(c):
        return jnp.maximum(c * halo_per_chunk - 1, 0)

    x_col0 = d_ssm // gw
    b_col0 = 2 * d_ssm // n
    c_col0 = b_col0 + n_groups
    cw_b0 = d_ssm // n
    cw_c0 = cw_b0 + n_groups
    vec128 = pl.BlockSpec((1, L), lambda g, c: (0, 0))

    def padv(v):
        return jnp.pad(v.astype(F32), (0, pad_h)).reshape(1, L)

    return pl.pallas_call(
        functools.partial(_ssd_kernel, n_heads=n_heads),
        grid=(n_groups, l // L),
        in_specs=[
            pl.BlockSpec((L, gw), lambda g, c: (c, g)),
            pl.BlockSpec((L, gw), lambda g, c: (c, x_col0 + g)),
            pl.BlockSpec((L, n), lambda g, c: (c, b_col0 + g)),
            pl.BlockSpec((L, n), lambda g, c: (c, c_col0 + g)),
            pl.BlockSpec((HALO_ROWS, gw), lambda g, c: (halo_idx(c), x_col0 + g)),
            pl.BlockSpec((HALO_ROWS, n), lambda g, c: (halo_idx(c), b_col0 + g)),
            pl.BlockSpec((HALO_ROWS, n), lambda g, c: (halo_idx(c), c_col0 + g)),
            pl.BlockSpec((L, L), lambda g, c: (c, 0)),
            pl.BlockSpec((CONV_K, gw), lambda g, c: (0, g)),
            pl.BlockSpec((1, gw), lambda g, c: (0, g)),
            pl.BlockSpec((CONV_K, n), lambda g, c: (0, cw_b0 + g)),
            pl.BlockSpec((1, n), lambda g, c: (0, cw_b0 + g)),
            pl.BlockSpec((CONV_K, n), lambda g, c: (0, cw_c0 + g)),
            pl.BlockSpec((1, n), lambda g, c: (0, cw_c0 + g)),
            vec128, vec128,
            pl.BlockSpec((1, gw), lambda g, c: (0, g)),
            pl.BlockSpec((1, gw), lambda g, c: (0, g)),
        ],
        out_specs=pl.BlockSpec((L, gw), lambda g, c: (c, g)),
        out_shape=jax.ShapeDtypeStruct((l, d_ssm), BF16),
        scratch_shapes=[pltpu.VMEM((n, gw), F32)],
        compiler_params=_params("parallel", "arbitrary"),
        name="ssd",
    )(proj_a, proj_a, proj_a, proj_a, proj_a, proj_a, proj_a, dt_raw,
      conv_w, conv_b.reshape(1, -1), conv_w, conv_b.reshape(1, -1), conv_w, conv_b.reshape(1, -1),
      padv(dt_bias), padv(a_log),
      jnp.repeat(d_skip.astype(F32), SSM_HEAD_DIM).reshape(1, d_ssm), ssm_norm.reshape(1, d_ssm))


def _attn_tiling(l):
    tq = _pick_tile(l, (640, 512, 256, 128))
    tk = tq
    worst_offset = 0 if tq % tk == 0 else tk - math.gcd(tq, tk)
    n_diag = -(-(worst_offset + tq) // tk)
    kv_rows = max(((i * tq) // tk + n_diag) * tk for i in range(l // tq))
    return tq, tk, n_diag, max(kv_rows, l)


def _lane_tiles(x):
    return [x[:, c * 128:(c + 1) * 128] for c in range(x.shape[1] // 128)]


def _attn_kernel(q_ref, k_ref, v_ref, slope_ref, lq1_ref, lk1_ref, lq2_ref, lk2_ref, subln_ref,
                 o_ref, m_ref, l_ref, acc_ref, *, tq, tk, n_diag):
    qi = pl.program_id(1)
    d = ATT_HEAD_DIM
    q0 = qi * tq
    slope = slope_ref[0, 0:1, 0:1]
    qpos = q0 + lax.broadcasted_iota(jnp.int32, (tq, 1), 0)

    m_ref[...] = jnp.full(m_ref.shape, NEG_BIG, F32)
    l_ref[...] = jnp.zeros(l_ref.shape, F32)
    acc_ref[...] = jnp.zeros(acc_ref.shape, F32)

    def step(jk, masked):
        ks = pl.multiple_of(jk * tk, tk)
        vblk = v_ref[pl.ds(ks, tk), :]
        kpos = ks + lax.broadcasted_iota(jnp.int32, (1, tk), 1)
        bias = jnp.where(kpos >= LEAD_PAD, slope * (kpos - q0).astype(F32), NEG_BIG)
        for i in range(2):
            s = lax.dot_general(q_ref[:, i * d:(i + 1) * d], k_ref[pl.ds(ks, tk), i * d:(i + 1) * d],
                                (((1,), (1,)), ((), ())), preferred_element_type=F32)
            s = s + bias
            if masked:
                s = jnp.where(kpos <= qpos, s, NEG_BIG)
            m_old = m_ref[i]
            m_new = jnp.maximum(m_old, jnp.max(functools.reduce(jnp.maximum, _lane_tiles(s)), axis=-1, keepdims=True))
            alpha = jnp.exp(m_old - m_new)
            p = jnp.exp(s - m_new[:, 0:1])
            l_ref[i] = alpha * l_ref[i] + functools.reduce(jnp.add, _lane_tiles(p))
            acc_ref[i] = (jnp.concatenate([alpha] * (2 * d // 128), axis=1) * acc_ref[i]
                          + jnp.dot(p.astype(BF16), vblk, preferred_element_type=F32))
            m_ref[i] = m_new

    n_full = q0 // tk

    def body(jk, carry):
        step(jk, False)
        return carry

    lax.fori_loop(0, n_full, body, 0)
    for jd in range(n_diag):
        step(n_full + jd, True)

    lam = (jnp.exp(jnp.sum(lq1_ref[...] * lk1_ref[...], axis=-1, keepdims=True))
           - jnp.exp(jnp.sum(lq2_ref[...] * lk2_ref[...], axis=-1, keepdims=True)) + LAMBDA_INIT)
    l0 = jnp.sum(l_ref[0], axis=-1, keepdims=True)
    l1 = jnp.sum(l_ref[1], axis=-1, keepdims=True)
    o = acc_ref[0] / l0 - lam * (acc_ref[1] / l1)
    o = _rms(o, SUBLN_EPS) * subln_ref[...] * (1.0 - LAMBDA_INIT)
    o_ref[...] = o.astype(o_ref.dtype)


def diff_attention(proj_b, l, n_heads, lam_q1, lam_k1, lam_q2, lam_k2, subln):
    hd = 2 * ATT_HEAD_DIM
    tq, tk, n_diag, kv_rows = _attn_tiling(l)
    assert proj_b.shape[0] == kv_rows
    slopes = 2.0 ** (-8.0 * jnp.arange(1, n_heads + 1, dtype=F32) / n_heads)
    slopes = jnp.broadcast_to(slopes[:, None, None], (n_heads, 8, 128))
    vec = pl.BlockSpec((1, ATT_HEAD_DIM), lambda h, i: (0, 0))
    return pl.pallas_call(
        functools.partial(_attn_kernel, tq=tq, tk=tk, n_diag=n_diag),
        grid=(n_heads, l // tq),
        in_specs=[
            pl.BlockSpec((tq, hd), lambda h, i: (i, h)),
            pl.BlockSpec((kv_rows, hd), lambda h, i: (0, n_heads + h)),
            pl.BlockSpec((kv_rows, hd), lambda h, i: (0, 2 * n_heads + h)),
            pl.BlockSpec((1, 8, 128), lambda h, i: (h, 0, 0)),
            vec, vec, vec, vec,
            pl.BlockSpec((1, hd), lambda h, i: (0, 0)),
        ],
        out_specs=pl.BlockSpec((tq, hd), lambda h, i: (i, h)),
        out_shape=jax.ShapeDtypeStruct((l, n_heads * hd), BF16),
        scratch_shapes=[pltpu.VMEM((2, tq, 128), F32), pltpu.VMEM((2, tq, 128), F32), pltpu.VMEM((2, tq, hd), F32)],
        compiler_params=_params("parallel", "arbitrary"),
        name="diff_attention",
    )(proj_b, proj_b, proj_b, slopes,
      lam_q1.reshape(1, -1), lam_k1.reshape(1, -1), lam_q2.reshape(1, -1), lam_k2.reshape(1, -1),
      subln.reshape(1, hd))


def kernel(x, meta_tokens, norm_pre_mix, w_in, conv_w, conv_b, dt_bias, a_log, d_skip, ssm_norm, w_ssm_out,
           lam_q1, lam_k1, lam_q2, lam_k2, subln, w_att_out, w_o, norm_post_mix, norm_pre_mlp, w_ff1, w_ff2,
           norm_post_mlp):
    batch, seq, d_model = x.shape
    assert batch == 1 and w_in.shape[0] == 1 and seq % ROW_BLOCK == 0
    d_ssm = w_ssm_out.shape[1]
    d_att = w_att_out.shape[1]
    n_ssm_heads = d_ssm // SSM_HEAD_DIM
    n_groups = n_ssm_heads // SSM_GROUP_HEADS
    n_att_heads = d_att // (2 * ATT_HEAD_DIM)
    assert n_ssm_heads <= ROW_BLOCK
    n_a = 2 * d_ssm + 2 * n_groups * D_STATE
    n_dt = n_ssm_heads
    assert w_in.shape[2] == n_a + n_dt + 3 * d_att + 2 * d_model
    l = ROW_BLOCK + seq
    kv_rows = _attn_tiling(l)[3]
    meta = meta_tokens.astype(F32)

    u = rms_cast(x[0], meta, norm_pre_mix[0], kv_rows)
    proj_a = matmul(u, w_in[0], BF16, m=l, n=n_a, name="in_proj_ssm")
    dt_raw = matmul(u, w_in[0], F32, m=l, col0=n_a, n=ROW_BLOCK, name="in_proj_dt")
    w_b = w_in[0, :, n_a + n_dt:].astype(BF16)
    scale = ATT_HEAD_DIM ** -0.5
    tn_b = _pick_tile(w_b.shape[1], (512, 256, 128))
    proj_b = matmul(u, w_b, BF16, name="in_proj_att",
                    epilogue=lambda acc, j: acc * jnp.where(j < d_att // tn_b, scale, 1.0))

    y_ssm = ssd_branch(proj_a, dt_raw, conv_w[0], conv_b[0], dt_bias[0], a_log[0], d_skip[0], ssm_norm[0],
                       d_ssm, n_groups)
    y_att = diff_attention(proj_b, l, n_att_heads, lam_q1[0], lam_k1[0], lam_q2[0], lam_k2[0], subln[0])

    mixed = gated_merge(y_ssm, y_att, w_ssm_out[0], w_att_out[0], proj_b, 3 * d_att)
    t = matmul(mixed, w_o[0], F32, name="out_proj")
    h1, u2 = post_mix(x[0], meta, t, norm_post_mix[0], norm_pre_mlp[0])

    f1 = matmul(u2, w_ff1[0], BF16, epilogue=lambda acc, j: jnp.square(jnp.maximum(acc, 0.0)), name="ff1")
    f2 = matmul_long_k(f1, w_ff2[0], name="ff2")
    out = post_mlp(h1, f2, norm_post_mlp[0], seq)
    return out[None]
```

```python
import functools
import math

import jax
import jax.numpy as jnp
from jax import lax
from jax.experimental import pallas as pl
from jax.experimental.pallas import tpu as pltpu

F32 = jnp.float32
BF16 = jnp.bfloat16

N_META = 16
ROW_BLOCK = 128
LEAD_PAD = ROW_BLOCK - N_META
SSM_HEAD_DIM = 64
SSM_GROUP_HEADS = 8
D_STATE = 128
CONV_K = 4
ATT_HEAD_DIM = 128
EPS = 1e-6
SUBLN_EPS = 1e-5
NEG_BIG = -1e30
LOG2_E = math.log2(math.e)
LAMBDA_INIT = 0.8 - 0.6 * math.exp(-0.3 * 0)
HALO_ROWS = 16

VMEM_LIMIT_BYTES = 56 * 1024 * 1024


def _params(*sem):
    return pltpu.CompilerParams(dimension_semantics=sem, vmem_limit_bytes=VMEM_LIMIT_BYTES)


def _pick_tile(n, candidates):
    for c in candidates:
        if n % c == 0:
            return c
    return n


M_TILES = (1664, 1408, 1280, 1024, 640, 512, 256, 128)


def _rms(x, eps):
    return x * lax.rsqrt(jnp.mean(x * x, axis=-1, keepdims=True) + eps)


def _h_block(i, x_ref, meta_ref):
    lead = jnp.concatenate([jnp.zeros((LEAD_PAD, meta_ref.shape[1]), F32), meta_ref[...]], axis=0)
    return jnp.where(i == 0, lead, x_ref[...])


def _h_specs(d, n_tok_blocks):
    return [pl.BlockSpec((ROW_BLOCK, d), lambda i: (jnp.clip(i - 1, 0, n_tok_blocks - 1), 0)),
            pl.BlockSpec((N_META, d), lambda i: (0, 0))]


def _rms_cast_kernel(x_ref, meta_ref, w_ref, o_ref, *, n_tok_blocks):
    i = pl.program_id(0)
    h = jnp.where(i > n_tok_blocks, 0.0, _h_block(i, x_ref, meta_ref))
    o_ref[...] = (_rms(h, EPS) * w_ref[...]).astype(o_ref.dtype)


def rms_cast(x, meta, w, rows):
    seq, d = x.shape
    n_tok_blocks = seq // ROW_BLOCK
    return pl.pallas_call(
        functools.partial(_rms_cast_kernel, n_tok_blocks=n_tok_blocks),
        grid=(rows // ROW_BLOCK,),
        in_specs=_h_specs(d, n_tok_blocks) + [pl.BlockSpec((1, d), lambda i: (0, 0))],
        out_specs=pl.BlockSpec((ROW_BLOCK, d), lambda i: (i, 0)),
        out_shape=jax.ShapeDtypeStruct((rows, d), BF16),
        compiler_params=_params("parallel"),
        name="rms_cast",
    )(x, meta, w.reshape(1, d))


def _post_mix_kernel(x_ref, meta_ref, t_ref, wpost_ref, wpre_ref, h1_ref, u_ref):
    h1 = _h_block(pl.program_id(0), x_ref, meta_ref) + _rms(t_ref[...], EPS) * wpost_ref[...]
    h1_ref[...] = h1
    u_ref[...] = (_rms(h1, EPS) * wpre_ref[...]).astype(u_ref.dtype)


def post_mix(x, meta, t, w_post, w_pre):
    seq, d = x.shape
    l = t.shape[0]
    row = pl.BlockSpec((ROW_BLOCK, d), lambda i: (i, 0))
    vec = pl.BlockSpec((1, d), lambda i: (0, 0))
    return pl.pallas_call(
        _post_mix_kernel,
        grid=(l // ROW_BLOCK,),
        in_specs=_h_specs(d, seq // ROW_BLOCK) + [row, vec, vec],
        out_specs=[row, row],
        out_shape=[jax.ShapeDtypeStruct((l, d), F32), jax.ShapeDtypeStruct((l, d), BF16)],
        compiler_params=_params("parallel"),
        name="post_mix",
    )(x, meta, t, w_post.reshape(1, d), w_pre.reshape(1, d))


def _post_mlp_kernel(h_ref, f_ref, w_ref, o_ref):
    o_ref[...] = h_ref[...] + _rms(f_ref[...], EPS) * w_ref[...]


def post_mlp(h1, f, w, seq):
    l, d = h1.shape
    tm = ROW_BLOCK
    skip = (l - seq) // tm
    row_in = pl.BlockSpec((tm, d), lambda i: (i + skip, 0))
    return pl.pallas_call(
        _post_mlp_kernel,
        grid=(seq // tm,),
        in_specs=[row_in, row_in, pl.BlockSpec((1, d), lambda i: (0, 0))],
        out_specs=pl.BlockSpec((tm, d), lambda i: (i, 0)),
        out_shape=jax.ShapeDtypeStruct((seq, d), F32),
        compiler_params=_params("parallel"),
        name="post_mlp",
    )(h1, f, w.reshape(1, d))


def _mm_kernel(x_ref, w_ref, *rest, epilogue, shift, transposed):
    if shift:
        wnext_ref, o_ref = rest
        if transposed:
            w = jnp.concatenate([w_ref[shift:, :].astype(BF16), wnext_ref[:shift, :].astype(BF16)], axis=0)
        else:
            w = jnp.concatenate([w_ref[:, shift:].astype(BF16), wnext_ref[:, :shift].astype(BF16)], axis=1)
    else:
        o_ref, = rest
        w = w_ref[...].astype(BF16)
    contract = (((1,), (1,)), ((), ())) if transposed else (((1,), (0,)), ((), ()))
    acc = lax.dot_general(x_ref[...], w, contract, preferred_element_type=F32)
    o_ref[...] = epilogue(acc, pl.program_id(1)).astype(o_ref.dtype)


def matmul(x, w, out_dtype, *, m=None, col0=0, n=None, transposed=False, epilogue=lambda acc, j: acc,
           name="matmul"):
    k = x.shape[1]
    m = x.shape[0] if m is None else m
    n = w.shape[1 if transposed else 2] - col0 if n is None else n
    shift = col0 % 128
    base = col0 - shift
    tm = _pick_tile(m, M_TILES)
    tn = _pick_tile(math.gcd(n, base) if base else n, (512, 256, 128))
    cb0 = base // tn
    next_blk = tn // 128
    if transposed:
        w_spec = pl.BlockSpec((None, tn, k), lambda i, j: (0, cb0 + j, 0))
        w_next_spec = pl.BlockSpec((None, 128, k), lambda i, j: (0, (cb0 + j + 1) * next_blk, 0))
    else:
        w_spec = pl.BlockSpec((None, k, tn), lambda i, j: (0, 0, cb0 + j))
        w_next_spec = pl.BlockSpec((None, k, 128), lambda i, j: (0, 0, (cb0 + j + 1) * next_blk))
    in_specs = [pl.BlockSpec((tm, k), lambda i, j: (i, 0), pipeline_mode=pl.Buffered(1)), w_spec]
    operands = [x, w]
    if shift:
        in_specs.append(w_next_spec)
        operands.append(w)
    return pl.pallas_call(
        functools.partial(_mm_kernel, epilogue=epilogue, shift=shift, transposed=transposed),
        grid=(m // tm, n // tn),
        in_specs=in_specs,
        out_specs=pl.BlockSpec((tm, tn), lambda i, j: (i, j)),
        out_shape=jax.ShapeDtypeStruct((m, n), out_dtype),
        compiler_params=_params("parallel", "arbitrary"),
        name=name,
    )(*operands)


def _mm_acc_kernel(x_ref, w_ref, o_ref):
    @pl.when(pl.program_id(2) == 0)
    def _():
        o_ref[...] = jnp.zeros(o_ref.shape, F32)

    o_ref[...] += jnp.dot(x_ref[...], w_ref[...].astype(BF16), preferred_element_type=F32)


def matmul_long_k(x, w, name="matmul_long_k"):
    m, k = x.shape
    n = w.shape[2]
    tm = _pick_tile(m, M_TILES)
    tn = _pick_tile(n, (1024, 512, 256, 128))
    tk = _pick_tile(k, (2048, 1024, 512))
    return pl.pallas_call(
        _mm_acc_kernel,
        grid=(m // tm, n // tn, k // tk),
        in_specs=[pl.BlockSpec((tm, tk), lambda i, j, kk: (i, kk)),
                  pl.BlockSpec((None, tk, tn), lambda i, j, kk: (0, kk, j))],
        out_specs=pl.BlockSpec((tm, tn), lambda i, j, kk: (i, j)),
        out_shape=jax.ShapeDtypeStruct((m, n), F32),
        compiler_params=_params("parallel", "arbitrary", "arbitrary"),
        name=name,
    )(x, w)


def _gated_merge_kernel(ys_ref, ya_ref, ws_ref, wa_ref, gs_ref, ga_ref, o_ref):
    a = jnp.dot(ys_ref[...], ws_ref[...].astype(BF16), preferred_element_type=F32)
    b = jnp.dot(ya_ref[...], wa_ref[...].astype(BF16), preferred_element_type=F32)
    gs = jax.nn.sigmoid(gs_ref[...].astype(F32))
    ga = jax.nn.sigmoid(ga_ref[...].astype(F32))
    o_ref[...] = (gs * a + ga * b).astype(o_ref.dtype)


def gated_merge(y_ssm, y_att, w_ssm, w_att, proj_b, gate_col0):
    m, k = y_ssm.shape
    ka = y_att.shape[1]
    n = w_ssm.shape[2]
    tm = _pick_tile(m, M_TILES)
    tn = _pick_tile(n, (256, 128))
    gs_blk = gate_col0 // tn
    ga_blk = (gate_col0 + n) // tn
    return pl.pallas_call(
        _gated_merge_kernel,
        grid=(m // tm, n // tn),
        in_specs=[pl.BlockSpec((tm, k), lambda i, j: (i, 0), pipeline_mode=pl.Buffered(1)),
                  pl.BlockSpec((tm, ka), lambda i, j: (i, 0), pipeline_mode=pl.Buffered(1)),
                  pl.BlockSpec((None, k, tn), lambda i, j: (0, 0, j)),
                  pl.BlockSpec((None, ka, tn), lambda i, j: (0, 0, j)),
                  pl.BlockSpec((tm, tn), lambda i, j: (i, gs_blk + j)),
                  pl.BlockSpec((tm, tn), lambda i, j: (i, ga_blk + j))],
        out_specs=pl.BlockSpec((tm, tn), lambda i, j: (i, j)),
        out_shape=jax.ShapeDtypeStruct((m, n), BF16),
        compiler_params=_params("parallel", "arbitrary"),
        name="gated_merge",
    )(y_ssm, y_att, w_ssm, w_att, proj_b, proj_b)


def _split_bf16(x, parts):
    out = []
    r = x
    for _ in range(parts):
        p = r.astype(BF16)
        out.append(p)
        r = r - p.astype(F32)
    return out


def _dot_onehot(x, onehot, parts, dims=(((1,), (0,)), ((), ()))):
    acc = None
    for p in _split_bf16(x, parts):
        t = lax.dot_general(p, onehot, dims, preferred_element_type=F32)
        acc = t if acc is None else acc + t
    return acc


def _causal_conv_silu(cur_ref, halo_ref, w_ref, b_ref, first_chunk):
    cur = cur_ref[...].astype(F32)
    halo = halo_ref[...].astype(F32)[HALO_ROWS - 8:, :]
    halo = jnp.where(first_chunk, 0.0, halo)
    row8 = lax.broadcasted_iota(jnp.int32, (8, 1), 0)
    acc = b_ref[...] + w_ref[CONV_K - 1:CONV_K, :] * cur
    for shift in range(1, CONV_K):
        rolled = pltpu.roll(cur, shift, 0)
        top = jnp.where(row8 < shift, pltpu.roll(halo, shift, 0), rolled[0:8, :])
        shifted = jnp.concatenate([top, rolled[8:, :]], axis=0)
        acc = acc + w_ref[CONV_K - 1 - shift:CONV_K - shift, :] * shifted
    return acc * jax.nn.sigmoid(acc)


def _ssd_kernel(z_ref, xs_ref, b_ref, c_ref, xsh_ref, bh_ref, ch_ref, dt_ref,
                cwx_ref, cbx_ref, cwb_ref, cbb_ref, cwc_ref, cbc_ref,
                dtb_ref, alog_ref, dskip_ref, nw_ref, o_ref, state_ref, *, n_heads):
    g = pl.program_id(0)
    c = pl.program_id(1)
    L = ROW_BLOCK
    gw = SSM_GROUP_HEADS * SSM_HEAD_DIM
    first = c == 0

    @pl.when(first)
    def _():
        state_ref[...] = jnp.zeros_like(state_ref)

    xs = _causal_conv_silu(xs_ref, xsh_ref, cwx_ref, cbx_ref, first)
    bm = _causal_conv_silu(b_ref, bh_ref, cwb_ref, cbb_ref, first)
    cm = _causal_conv_silu(c_ref, ch_ref, cwc_ref, cbc_ref, first)

    dt = jax.nn.softplus(dt_ref[...] + dtb_ref[...])
    row = c * L + lax.broadcasted_iota(jnp.int32, (L, 1), 0)
    lane = lax.broadcasted_iota(jnp.int32, (1, L), 1)
    dt = jnp.where((row >= LEAD_PAD) & (lane < n_heads), dt, 0.0)
    adt = dt * (-jnp.exp(alog_ref[...]))
    r_i = lax.broadcasted_iota(jnp.int32, (L, L), 0)
    c_i = lax.broadcasted_iota(jnp.int32, (L, L), 1)
    causal = r_i >= c_i
    tri = causal.astype(BF16)
    adt_parts = _split_bf16(adt, 3)
    a_cum = sum(jnp.dot(tri, p, preferred_element_type=F32) for p in adt_parts)
    pick = (g * SSM_GROUP_HEADS + lax.broadcasted_iota(jnp.int32, (HALO_ROWS, L), 0)
            == lax.broadcasted_iota(jnp.int32, (HALO_ROWS, L), 1)).astype(BF16)
    a_t_grp = sum(lax.dot_general(pick, p, (((1,), (1,)), ((), ())), preferred_element_type=F32)
                  for p in _split_bf16(a_cum, 3))

    hl = lax.broadcasted_iota(jnp.int32, (L, gw), 0)
    ch_head = g * SSM_GROUP_HEADS + lax.broadcasted_iota(jnp.int32, (L, gw), 1) // SSM_HEAD_DIM
    spread = (hl == ch_head).astype(BF16)
    hl2 = lax.broadcasted_iota(jnp.int32, (L, SSM_GROUP_HEADS * L), 0)
    col_head = g * SSM_GROUP_HEADS + lax.broadcasted_iota(jnp.int32, (L, SSM_GROUP_HEADS * L), 1) // L
    spread_wide = (hl2 == col_head).astype(BF16)

    a_ch = _dot_onehot(a_cum, spread, 3)
    dt_ch = _dot_onehot(dt, spread, 3)
    a_col = _dot_onehot(a_cum, spread_wide, 3)

    xdt = xs * dt_ch
    a_last = a_ch[L - 1:L, :]
    cm_b = cm.astype(BF16)
    bm_b = bm.astype(BF16)
    cb = lax.dot_general(cm_b, bm_b, (((1,), (1,)), ((), ())), preferred_element_type=F32)

    state = state_ref[...]
    y = jnp.dot(cm_b, state.astype(BF16), preferred_element_type=F32) * jnp.exp(a_ch)
    decayed = (xdt * jnp.exp(a_last - a_ch)).astype(BF16)
    state_ref[...] = state * jnp.exp(a_last) + jnp.dot(bm.T.astype(BF16), decayed, preferred_element_type=F32)

    xdt_b = xdt.astype(BF16)
    lane_lo = lax.broadcasted_iota(jnp.int32, (1, 2 * SSM_HEAD_DIM), 1) < SSM_HEAD_DIM
    y_pairs = []
    for pp in range(SSM_GROUP_HEADS // 2):
        x_pair = xdt_b[:, pp * 2 * SSM_HEAD_DIM:(pp + 1) * 2 * SSM_HEAD_DIM]
        y_pair = None
        for half in range(2):
            hh = 2 * pp + half
            seg = a_col[:, hh * L:(hh + 1) * L] - a_t_grp[hh:hh + 1, :]
            decay = jnp.exp(jnp.where(causal, seg, -jnp.inf))
            lhs = (cb * decay).astype(BF16)
            rhs = jnp.where(lane_lo if half == 0 else jnp.logical_not(lane_lo), x_pair, jnp.zeros_like(x_pair))
            t = jnp.dot(lhs, rhs, preferred_element_type=F32)
            y_pair = t if y_pair is None else y_pair + t
        y_pairs.append(y_pair)
    y = y + jnp.concatenate(y_pairs, axis=1) + xs * dskip_ref[...]

    z = z_ref[...].astype(F32)
    gt = y * (z * jax.nn.sigmoid(z))
    o_ref[...] = (_rms(gt, EPS) * nw_ref[...]).astype(o_ref.dtype)


def ssd_branch(proj_a, dt_raw, conv_w, conv_b, dt_bias, a_log, d_skip, ssm_norm, d_ssm, n_groups):
    l = proj_a.shape[0]
    L = ROW_BLOCK
    gw = SSM_GROUP_HEADS * SSM_HEAD_DIM
    n = D_STATE
    n_heads = d_ssm // SSM_HEAD_DIM
    pad_h = L - n_heads
    halo_per_chunk = L // HALO_ROWS

    def halo_idx(c):
        return jnp.maximum(c * halo_per_chunk - 1, 0)

    x_col0 = d_ssm // gw
    b_col0 = 2 * d_ssm // n
    c_col0 = b_col0 + n_groups
    cw_b0 = d_ssm // n
    cw_c0 = cw_b0 + n_groups
    vec128 = pl.BlockSpec((1, L), lambda g, c: (0, 0))

    def padv(v):
        return jnp.pad(v.astype(F32), (0, pad_h)).reshape(1, L)

    return pl.pallas_call(
        functools.partial(_ssd_kernel, n_heads=n_heads),
        grid=(n_groups, l // L),
        in_specs=[
            pl.BlockSpec((L, gw), lambda g, c: (c, g)),
            pl.BlockSpec((L, gw), lambda g, c: (c, x_col0 + g)),
            pl.BlockSpec((L, n), lambda g, c: (c, b_col0 + g)),
            pl.BlockSpec((L, n), lambda g, c: (c, c_col0 + g)),
            pl.BlockSpec((HALO_ROWS, gw), lambda g, c: (halo_idx(c), x_col0 + g)),
            pl.BlockSpec((HALO_ROWS, n), lambda g, c: (halo_idx(c), b_col0 + g)),
            pl.BlockSpec((HALO_ROWS, n), lambda g, c: (halo_idx(c), c_col0 + g)),
            pl.BlockSpec((L, L), lambda g, c: (c, 0)),
            pl.BlockSpec((CONV_K, gw), lambda g, c: (0, g)),
            pl.BlockSpec((1, gw), lambda g, c: (0, g)),
            pl.BlockSpec((CONV_K, n), lambda g, c: (0, cw_b0 + g)),
            pl.BlockSpec((1, n), lambda g, c: (0, cw_b0 + g)),
            pl.BlockSpec((CONV_K, n), lambda g, c: (0, cw_c0 + g)),
            pl.BlockSpec((1, n), lambda g, c: (0, cw_c0 + g)),
            vec128, vec128,
            pl.BlockSpec((1, gw), lambda g, c: (0, g)),
            pl.BlockSpec((1, gw), lambda g, c: (0, g)),
        ],
        out_specs=pl.BlockSpec((L, gw), lambda g, c: (c, g)),
        out_shape=jax.ShapeDtypeStruct((l, d_ssm), BF16),
        scratch_shapes=[pltpu.VMEM((n, gw), F32)],
        compiler_params=_params("parallel", "arbitrary"),
        name="ssd",
    )(proj_a, proj_a, proj_a, proj_a, proj_a, proj_a, proj_a, dt_raw,
      conv_w, conv_b.reshape(1, -1), conv_w, conv_b.reshape(1, -1), conv_w, conv_b.reshape(1, -1),
      padv(dt_bias), padv(a_log),
      jnp.repeat(d_skip.astype(F32), SSM_HEAD_DIM).reshape(1, d_ssm), ssm_norm.reshape(1, d_ssm))


def _lane_tiles(x):
    return [x[:, c * 128:(c + 1) * 128] for c in range(x.shape[1] // 128)]


def _attn_kernel(q_ref, k_ref, v_ref, slope_ref, lq1_ref, lk1_ref, lq2_ref, lk2_ref, subln_ref,
                 o_ref, m_ref, l_ref, acc_ref, sa_ref, sb_ref, *, tq):
    qi = pl.program_id(1)
    d = ATT_HEAD_DIM
    q0 = qi * tq
    slope = slope_ref[0, 0:1, 0:1]
    qpos = q0 + lax.broadcasted_iota(jnp.int32, (tq, 1), 0)

    m_ref[...] = jnp.full(m_ref.shape, NEG_BIG, F32)
    l_ref[...] = jnp.zeros(l_ref.shape, F32)
    acc_ref[...] = jnp.zeros(acc_ref.shape, F32)

    def scores(j, s_ref):
        ks = pl.multiple_of(j * tq, tq)
        kpos = ks + lax.broadcasted_iota(jnp.int32, (1, tq), 1)
        bias = jnp.where(kpos >= LEAD_PAD, slope * (kpos - q0).astype(F32), NEG_BIG)
        for i in range(2):
            s = lax.dot_general(q_ref[:, i * d:(i + 1) * d], k_ref[pl.ds(ks, tq), i * d:(i + 1) * d],
                                (((1,), (1,)), ((), ())), preferred_element_type=F32)
            s_ref[i] = s + bias

    def consume(j, s_ref, masked):
        ks = pl.multiple_of(j * tq, tq)
        vblk = v_ref[pl.ds(ks, tq), :]
        kpos = ks + lax.broadcasted_iota(jnp.int32, (1, tq), 1)
        for i in range(2):
            s = s_ref[i]
            if masked:
                s = jnp.where(kpos <= qpos, s, NEG_BIG)
            m_old = m_ref[i]
            m_new = jnp.maximum(m_old, jnp.max(functools.reduce(jnp.maximum, _lane_tiles(s)), axis=-1, keepdims=True))
            alpha = jnp.exp2(m_old - m_new)
            p = jnp.exp2(s - m_new[:, 0:1])
            l_ref[i] = alpha * l_ref[i] + functools.reduce(jnp.add, _lane_tiles(p))
            acc_ref[i] = (jnp.concatenate([alpha] * (2 * d // 128), axis=1) * acc_ref[i]
                          + jnp.dot(p.astype(BF16), vblk, preferred_element_type=F32))
            m_ref[i] = m_new

    n_chunks = qi + 1
    scores(0, sa_ref)

    def pair(t, carry):
        scores(2 * t + 1, sb_ref)
        consume(2 * t, sa_ref, False)
        scores(2 * t + 2, sa_ref)
        consume(2 * t + 1, sb_ref, False)
        return carry

    lax.fori_loop(0, (n_chunks - 1) // 2, pair, 0)

    @pl.when(n_chunks % 2 == 1)
    def _():
        consume(n_chunks - 1, sa_ref, True)

    @pl.when(n_chunks % 2 == 0)
    def _():
        scores(n_chunks - 1, sb_ref)
        consume(n_chunks - 2, sa_ref, False)
        consume(n_chunks - 1, sb_ref, True)

    lam = (jnp.exp(jnp.sum(lq1_ref[...] * lk1_ref[...], axis=-1, keepdims=True))
           - jnp.exp(jnp.sum(lq2_ref[...] * lk2_ref[...], axis=-1, keepdims=True)) + LAMBDA_INIT)
    l0 = jnp.sum(l_ref[0], axis=-1, keepdims=True)
    l1 = jnp.sum(l_ref[1], axis=-1, keepdims=True)
    o = acc_ref[0] / l0 - lam * (acc_ref[1] / l1)
    o = _rms(o, SUBLN_EPS) * subln_ref[...] * (1.0 - LAMBDA_INIT)
    o_ref[...] = o.astype(o_ref.dtype)


def diff_attention(proj_b, l, n_heads, lam_q1, lam_k1, lam_q2, lam_k2, subln):
    hd = 2 * ATT_HEAD_DIM
    assert proj_b.shape[0] == l
    tq = _pick_tile(l, (640, 512, 256, 128))
    slopes = LOG2_E * 2.0 ** (-8.0 * jnp.arange(1, n_heads + 1, dtype=F32) / n_heads)
    slopes = jnp.broadcast_to(slopes[:, None, None], (n_heads, 8, 128))
    vec = pl.BlockSpec((1, ATT_HEAD_DIM), lambda h, i: (0, 0))
    return pl.pallas_call(
        functools.partial(_attn_kernel, tq=tq),
        grid=(n_heads, l // tq),
        in_specs=[
            pl.BlockSpec((tq, hd), lambda h, i: (i, h)),
            pl.BlockSpec((l, hd), lambda h, i: (0, n_heads + h)),
            pl.BlockSpec((l, hd), lambda h, i: (0, 2 * n_heads + h)),
            pl.BlockSpec((1, 8, 128), lambda h, i: (h, 0, 0)),
            vec, vec, vec, vec,
            pl.BlockSpec((1, hd), lambda h, i: (0, 0)),
        ],
        out_specs=pl.BlockSpec((tq, hd), lambda h, i: (i, h)),
        out_shape=jax.ShapeDtypeStruct((l, n_heads * hd), BF16),
        scratch_shapes=[pltpu.VMEM((2, tq, 128), F32), pltpu.VMEM((2, tq, 128), F32), pltpu.VMEM((2, tq, hd), F32),
                        pltpu.VMEM((2, tq, tq), F32), pltpu.VMEM((2, tq, tq), F32)],
        compiler_params=_params("parallel", "arbitrary"),
        name="diff_attention",
    )(proj_b, proj_b, proj_b, slopes,
      lam_q1.reshape(1, -1), lam_k1.reshape(1, -1), lam_q2.reshape(1, -1), lam_k2.reshape(1, -1),
      subln.reshape(1, hd))


def kernel(x, meta_tokens, norm_pre_mix, w_in, conv_w, conv_b, dt_bias, a_log, d_skip, ssm_norm, w_ssm_out,
           lam_q1, lam_k1, lam_q2, lam_k2, subln, w_att_out, w_o, norm_post_mix, norm_pre_mlp, w_ff1, w_ff2,
           norm_post_mlp):
    batch, seq, d_model = x.shape
    assert batch == 1 and w_in.shape[0] == 1 and seq % ROW_BLOCK == 0
    d_ssm = w_ssm_out.shape[1]
    d_att = w_att_out.shape[1]
    n_ssm_heads = d_ssm // SSM_HEAD_DIM
    n_groups = n_ssm_heads // SSM_GROUP_HEADS
    n_att_heads = d_att // (2 * ATT_HEAD_DIM)
    assert n_ssm_heads <= ROW_BLOCK
    n_a = 2 * d_ssm + 2 * n_groups * D_STATE
    n_dt = n_ssm_heads
    assert w_in.shape[2] == n_a + n_dt + 3 * d_att + 2 * d_model
    l = ROW_BLOCK + seq
    meta = meta_tokens.astype(F32)

    u = rms_cast(x[0], meta, norm_pre_mix[0], l)
    w_in_t = jnp.swapaxes(w_in, 1, 2)
    proj_a = matmul(u, w_in_t, BF16, m=l, n=n_a, transposed=True, name="in_proj_ssm")
    dt_raw = matmul(u, w_in_t, F32, m=l, col0=n_a, n=ROW_BLOCK, transposed=True, name="in_proj_dt")
    q_scale = ATT_HEAD_DIM ** -0.5 * LOG2_E
    n_b = 3 * d_att + 2 * d_model
    tn_b = _pick_tile(math.gcd(n_b, n_a), (512, 256, 128))
    proj_b = matmul(u, w_in_t, BF16, col0=n_a + n_dt, n=n_b, transposed=True, name="in_proj_att",
                    epilogue=lambda acc, j: acc * jnp.where(j < d_att // tn_b, q_scale, 1.0))

    y_ssm = ssd_branch(proj_a, dt_raw, conv_w[0], conv_b[0], dt_bias[0], a_log[0], d_skip[0], ssm_norm[0],
                       d_ssm, n_groups)
    y_att = diff_attention(proj_b, l, n_att_heads, lam_q1[0], lam_k1[0], lam_q2[0], lam_k2[0], subln[0])

    mixed = gated_merge(y_ssm, y_att, w_ssm_out, w_att_out, proj_b, 3 * d_att)
    t = matmul(mixed, w_o, F32, name="out_proj")
    h1, u2 = post_mix(x[0], meta, t, norm_post_mix[0], norm_pre_mlp[0])

    f1 = matmul(u2, w_ff1, BF16, epilogue=lambda acc, j: jnp.square(jnp.maximum(acc, 0.0)), name="ff1")
    f2 = matmul_long_k(f1, w_ff2, name="ff2")
    out = post_mlp(h1, f2, norm_post_mlp[0], seq)
    return out[None]
```

```python
import functools
import math

import jax
import jax.numpy as jnp
from jax import lax
from jax.experimental import pallas as pl
from jax.experimental.pallas import tpu as pltpu

F32 = jnp.float32
BF16 = jnp.bfloat16

N_META = 16
ROW_BLOCK = 128
LEAD_PAD = ROW_BLOCK - N_META
SSM_HEAD_DIM = 64
SSM_GROUP_HEADS = 8
D_STATE = 128
CONV_K = 4
ATT_HEAD_DIM = 128
EPS = 1e-6
SUBLN_EPS = 1e-5
NEG_BIG = -1e30
LOG2_E = math.log2(math.e)
LAMBDA_INIT = 0.8 - 0.6 * math.exp(-0.3 * 0)
HALO_ROWS = 16
SSD_GROUPS_PER_STEP = (8, 4, 2, 1)
ATT_UNROLL = 4

VMEM_LIMIT_BYTES = 56 * 1024 * 1024


def _params(*sem):
    return pltpu.CompilerParams(dimension_semantics=sem, vmem_limit_bytes=VMEM_LIMIT_BYTES)


def _pick_tile(n, candidates):
    for c in candidates:
        if n % c == 0:
            return c
    return n


M_TILES = (1664, 1408, 1280, 1024, 640, 512, 256, 128)


def _rms(x, eps):
    return x * lax.rsqrt(jnp.mean(x * x, axis=-1, keepdims=True) + eps)


def _h_block(i, x_ref, meta_ref):
    lead = jnp.concatenate([jnp.zeros((LEAD_PAD, meta_ref.shape[1]), F32), meta_ref[...]], axis=0)
    return jnp.where(i == 0, lead, x_ref[...])


def _h_specs(d, n_tok_blocks):
    return [pl.BlockSpec((ROW_BLOCK, d), lambda i: (jnp.clip(i - 1, 0, n_tok_blocks - 1), 0)),
            pl.BlockSpec((N_META, d), lambda i: (0, 0))]


def _rms_cast_kernel(x_ref, meta_ref, w_ref, o_ref, *, n_tok_blocks):
    i = pl.program_id(0)
    h = jnp.where(i > n_tok_blocks, 0.0, _h_block(i, x_ref, meta_ref))
    o_ref[...] = (_rms(h, EPS) * w_ref[...]).astype(o_ref.dtype)


def rms_cast(x, meta, w, rows):
    seq, d = x.shape
    n_tok_blocks = seq // ROW_BLOCK
    return pl.pallas_call(
        functools.partial(_rms_cast_kernel, n_tok_blocks=n_tok_blocks),
        grid=(rows // ROW_BLOCK,),
        in_specs=_h_specs(d, n_tok_blocks) + [pl.BlockSpec((1, d), lambda i: (0, 0))],
        out_specs=pl.BlockSpec((ROW_BLOCK, d), lambda i: (i, 0)),
        out_shape=jax.ShapeDtypeStruct((rows, d), BF16),
        compiler_params=_params("parallel"),
        name="rms_cast",
    )(x, meta, w.reshape(1, d))


def _post_mix_kernel(x_ref, meta_ref, t_ref, wpost_ref, wpre_ref, h1_ref, u_ref):
    h1 = _h_block(pl.program_id(0), x_ref, meta_ref) + _rms(t_ref[...].astype(F32), EPS) * wpost_ref[...]
    h1_ref[...] = h1
    u_ref[...] = (_rms(h1, EPS) * wpre_ref[...]).astype(u_ref.dtype)


def post_mix(x, meta, t, w_post, w_pre):
    seq, d = x.shape
    l = t.shape[0]
    row = pl.BlockSpec((ROW_BLOCK, d), lambda i: (i, 0))
    vec = pl.BlockSpec((1, d), lambda i: (0, 0))
    return pl.pallas_call(
        _post_mix_kernel,
        grid=(l // ROW_BLOCK,),
        in_specs=_h_specs(d, seq // ROW_BLOCK) + [row, vec, vec],
        out_specs=[row, row],
        out_shape=[jax.ShapeDtypeStruct((l, d), F32), jax.ShapeDtypeStruct((l, d), BF16)],
        compiler_params=_params("parallel"),
        name="post_mix",
    )(x, meta, t, w_post.reshape(1, d), w_pre.reshape(1, d))


def _post_mlp_kernel(h_ref, f_ref, w_ref, o_ref):
    o_ref[...] = h_ref[...] + _rms(f_ref[...].astype(F32), EPS) * w_ref[...]


def post_mlp(h1, f, w, seq):
    l, d = h1.shape
    tm = ROW_BLOCK
    skip = (l - seq) // tm
    row_in = pl.BlockSpec((tm, d), lambda i: (i + skip, 0))
    return pl.pallas_call(
        _post_mlp_kernel,
        grid=(seq // tm,),
        in_specs=[row_in, row_in, pl.BlockSpec((1, d), lambda i: (0, 0))],
        out_specs=pl.BlockSpec((tm, d), lambda i: (i, 0)),
        out_shape=jax.ShapeDtypeStruct((seq, d), F32),
        compiler_params=_params("parallel"),
        name="post_mlp",
    )(h1, f, w.reshape(1, d))


def _mm_kernel(x_ref, w_ref, *rest, epilogue, shift, transposed):
    if shift:
        wnext_ref, o_ref = rest
        if transposed:
            w = jnp.concatenate([w_ref[shift:, :].astype(BF16), wnext_ref[:shift, :].astype(BF16)], axis=0)
        else:
            w = jnp.concatenate([w_ref[:, shift:].astype(BF16), wnext_ref[:, :shift].astype(BF16)], axis=1)
    else:
        o_ref, = rest
        w = w_ref[...].astype(BF16)
    contract = (((1,), (1,)), ((), ())) if transposed else (((1,), (0,)), ((), ()))
    acc = lax.dot_general(x_ref[...], w, contract, preferred_element_type=F32)
    o_ref[...] = epilogue(acc, pl.program_id(1)).astype(o_ref.dtype)


def matmul(x, w, out_dtype, *, m=None, col0=0, n=None, transposed=False, epilogue=lambda acc, j: acc,
           name="matmul"):
    k = x.shape[1]
    m = x.shape[0] if m is None else m
    n = w.shape[1 if transposed else 2] - col0 if n is None else n
    shift = col0 % 128
    base = col0 - shift
    tm = _pick_tile(m, M_TILES)
    tn = _pick_tile(math.gcd(n, base) if base else n, (512, 256, 128))
    cb0 = base // tn
    next_blk = tn // 128
    if transposed:
        w_spec = pl.BlockSpec((None, tn, k), lambda i, j: (0, cb0 + j, 0))
        w_next_spec = pl.BlockSpec((None, 128, k), lambda i, j: (0, (cb0 + j + 1) * next_blk, 0))
    else:
        w_spec = pl.BlockSpec((None, k, tn), lambda i, j: (0, 0, cb0 + j))
        w_next_spec = pl.BlockSpec((None, k, 128), lambda i, j: (0, 0, (cb0 + j + 1) * next_blk))
    in_specs = [pl.BlockSpec((tm, k), lambda i, j: (i, 0), pipeline_mode=pl.Buffered(1)), w_spec]
    operands = [x, w]
    if shift:
        in_specs.append(w_next_spec)
        operands.append(w)
    return pl.pallas_call(
        functools.partial(_mm_kernel, epilogue=epilogue, shift=shift, transposed=transposed),
        grid=(m // tm, n // tn),
        in_specs=in_specs,
        out_specs=pl.BlockSpec((tm, tn), lambda i, j: (i, j)),
        out_shape=jax.ShapeDtypeStruct((m, n), out_dtype),
        compiler_params=_params("parallel", "arbitrary"),
        name=name,
    )(*operands)


def _mm_acc_kernel(x_ref, w_ref, o_ref, acc_ref):
    kk = pl.program_id(2)

    @pl.when(kk == 0)
    def _():
        acc_ref[...] = jnp.zeros(acc_ref.shape, F32)

    acc_ref[...] += jnp.dot(x_ref[...], w_ref[...].astype(BF16), preferred_element_type=F32)

    @pl.when(kk == pl.num_programs(2) - 1)
    def _():
        o_ref[...] = acc_ref[...].astype(o_ref.dtype)


def matmul_long_k(x, w, out_dtype, name="matmul_long_k"):
    m, k = x.shape
    n = w.shape[2]
    tm = _pick_tile(m, M_TILES)
    tn = _pick_tile(n, (1024, 512, 256, 128))
    tk = _pick_tile(k, (2048, 1024, 512))
    return pl.pallas_call(
        _mm_acc_kernel,
        grid=(m // tm, n // tn, k // tk),
        in_specs=[pl.BlockSpec((tm, tk), lambda i, j, kk: (i, kk)),
                  pl.BlockSpec((None, tk, tn), lambda i, j, kk: (0, kk, j))],
        out_specs=pl.BlockSpec((tm, tn), lambda i, j, kk: (i, j)),
        out_shape=jax.ShapeDtypeStruct((m, n), out_dtype),
        scratch_shapes=[pltpu.VMEM((tm, tn), F32)],
        compiler_params=_params("parallel", "arbitrary", "arbitrary"),
        name=name,
    )(x, w)


def _gated_merge_kernel(ys_ref, ya_ref, ws_ref, wa_ref, gs_ref, ga_ref, o_ref):
    a = jnp.dot(ys_ref[...], ws_ref[...].astype(BF16), preferred_element_type=F32)
    b = jnp.dot(ya_ref[...], wa_ref[...].astype(BF16), preferred_element_type=F32)
    gs = jax.nn.sigmoid(gs_ref[...].astype(F32))
    ga = jax.nn.sigmoid(ga_ref[...].astype(F32))
    o_ref[...] = (gs * a + ga * b).astype(o_ref.dtype)


def gated_merge(y_ssm, y_att, w_ssm, w_att, proj_b, gate_col0):
    m, k = y_ssm.shape
    ka = y_att.shape[1]
    n = w_ssm.shape[2]
    tm = _pick_tile(m, M_TILES)
    tn = _pick_tile(n, (256, 128))
    gs_blk = gate_col0 // tn
    ga_blk = (gate_col0 + n) // tn
    return pl.pallas_call(
        _gated_merge_kernel,
        grid=(m // tm, n // tn),
        in_specs=[pl.BlockSpec((tm, k), lambda i, j: (i, 0), pipeline_mode=pl.Buffered(1)),
                  pl.BlockSpec((tm, ka), lambda i, j: (i, 0), pipeline_mode=pl.Buffered(1)),
                  pl.BlockSpec((None, k, tn), lambda i, j: (0, 0, j)),
                  pl.BlockSpec((None, ka, tn), lambda i, j: (0, 0, j)),
                  pl.BlockSpec((tm, tn), lambda i, j: (i, gs_blk + j)),
                  pl.BlockSpec((tm, tn), lambda i, j: (i, ga_blk + j))],
        out_specs=pl.BlockSpec((tm, tn), lambda i, j: (i, j)),
        out_shape=jax.ShapeDtypeStruct((m, n), BF16),
        compiler_params=_params("parallel", "arbitrary"),
        name="gated_merge",
    )(y_ssm, y_att, w_ssm, w_att, proj_b, proj_b)


def _split_bf16(x, parts):
    out = []
    r = x
    for _ in range(parts):
        p = r.astype(BF16)
        out.append(p)
        r = r - p.astype(F32)
    return out


def _dot_onehot(x, onehot, parts, dims=(((1,), (0,)), ((), ()))):
    acc = None
    for p in _split_bf16(x, parts):
        t = lax.dot_general(p, onehot, dims, preferred_element_type=F32)
        acc = t if acc is None else acc + t
    return acc


def _causal_conv_silu(cur_ref, halo_ref, w_ref, b_ref, first_chunk):
    cur = cur_ref[...].astype(F32)
    halo = halo_ref[...].astype(F32)[HALO_ROWS - 8:, :]
    halo = jnp.where(first_chunk, 0.0, halo)
    row8 = lax.broadcasted_iota(jnp.int32, (8, 1), 0)
    acc = b_ref[...] + w_ref[CONV_K - 1:CONV_K, :] * cur
    for shift in range(1, CONV_K):
        rolled = pltpu.roll(cur, shift, 0)
        top = jnp.where(row8 < shift, pltpu.roll(halo, shift, 0), rolled[0:8, :])
        shifted = jnp.concatenate([top, rolled[8:, :]], axis=0)
        acc = acc + w_ref[CONV_K - 1 - shift:CONV_K - shift, :] * shifted
    return acc * jax.nn.sigmoid(acc)


def _ssd_kernel(z_ref, xs_ref, b_ref, c_ref, xsh_ref, bh_ref, ch_ref, dt_ref,
                cwx_ref, cbx_ref, cwb_ref, cbb_ref, cwc_ref, cbc_ref,
                dtb_ref, alog_ref, dskip_ref, nw_ref, o_ref, state_ref, *, n_heads, gps):
    g0 = pl.program_id(0) * gps
    c = pl.program_id(1)
    L = ROW_BLOCK
    gw = SSM_GROUP_HEADS * SSM_HEAD_DIM
    n = D_STATE
    first = c == 0

    @pl.when(first)
    def _():
        state_ref[...] = jnp.zeros_like(state_ref)

    xs_all = _causal_conv_silu(xs_ref, xsh_ref, cwx_ref, cbx_ref, first)
    bm_all = _causal_conv_silu(b_ref, bh_ref, cwb_ref, cbb_ref, first)
    cm_all = _causal_conv_silu(c_ref, ch_ref, cwc_ref, cbc_ref, first)

    dt = jax.nn.softplus(dt_ref[...] + dtb_ref[...])
    row = c * L + lax.broadcasted_iota(jnp.int32, (L, 1), 0)
    lane = lax.broadcasted_iota(jnp.int32, (1, L), 1)
    dt = jnp.where((row >= LEAD_PAD) & (lane < n_heads), dt, 0.0)
    adt = dt * (-jnp.exp(alog_ref[...]))
    r_i = lax.broadcasted_iota(jnp.int32, (L, L), 0)
    c_i = lax.broadcasted_iota(jnp.int32, (L, L), 1)
    causal = r_i >= c_i
    tri = causal.astype(BF16)
    adt_parts = _split_bf16(adt, 3)
    a_cum = sum(jnp.dot(tri, p, preferred_element_type=F32) for p in adt_parts)
    n_pick = max(HALO_ROWS, gps * SSM_GROUP_HEADS)
    pick = (g0 * SSM_GROUP_HEADS + lax.broadcasted_iota(jnp.int32, (n_pick, L), 0)
            == lax.broadcasted_iota(jnp.int32, (n_pick, L), 1)).astype(BF16)
    a_t = sum(lax.dot_general(pick, p, (((1,), (1,)), ((), ())), preferred_element_type=F32)
              for p in _split_bf16(a_cum, 3))

    hl = lax.broadcasted_iota(jnp.int32, (L, gps * gw), 0)
    ch_head = g0 * SSM_GROUP_HEADS + lax.broadcasted_iota(jnp.int32, (L, gps * gw), 1) // SSM_HEAD_DIM
    spread = (hl == ch_head).astype(BF16)
    wide = gps * SSM_GROUP_HEADS * L
    hl2 = lax.broadcasted_iota(jnp.int32, (L, wide), 0)
    col_head = g0 * SSM_GROUP_HEADS + lax.broadcasted_iota(jnp.int32, (L, wide), 1) // L
    spread_wide = (hl2 == col_head).astype(BF16)

    a_ch_all = _dot_onehot(a_cum, spread, 3)
    dt_ch_all = _dot_onehot(dt, spread, 3)
    a_col_all = _dot_onehot(a_cum, spread_wide, 3)

    lane_lo = lax.broadcasted_iota(jnp.int32, (1, 2 * SSM_HEAD_DIM), 1) < SSM_HEAD_DIM
    for gg in range(gps):
        ch = slice(gg * gw, (gg + 1) * gw)
        st = slice(gg * n, (gg + 1) * n)
        xs, a_ch = xs_all[:, ch], a_ch_all[:, ch]
        xdt = xs * dt_ch_all[:, ch]
        a_last = a_ch[L - 1:L, :]
        bm = bm_all[:, st]
        cm_b = cm_all[:, st].astype(BF16)
        cb = lax.dot_general(cm_b, bm.astype(BF16), (((1,), (1,)), ((), ())), preferred_element_type=F32)

        state = state_ref[gg]
        y = jnp.dot(cm_b, state.astype(BF16), preferred_element_type=F32) * jnp.exp(a_ch)
        decayed = (xdt * jnp.exp(a_last - a_ch)).astype(BF16)
        state_ref[gg] = state * jnp.exp(a_last) + jnp.dot(bm.T.astype(BF16), decayed, preferred_element_type=F32)

        xdt_b = xdt.astype(BF16)
        y_pairs = []
        for pp in range(SSM_GROUP_HEADS // 2):
            x_pair = xdt_b[:, pp * 2 * SSM_HEAD_DIM:(pp + 1) * 2 * SSM_HEAD_DIM]
            y_pair = None
            for half in range(2):
                hh = gg * SSM_GROUP_HEADS + 2 * pp + half
                seg = a_col_all[:, hh * L:(hh + 1) * L] - a_t[hh:hh + 1, :]
                decay = jnp.exp(jnp.where(causal, seg, -jnp.inf))
                lhs = (cb * decay).astype(BF16)
                rhs = jnp.where(lane_lo if half == 0 else jnp.logical_not(lane_lo), x_pair, jnp.zeros_like(x_pair))
                t = jnp.dot(lhs, rhs, preferred_element_type=F32)
                y_pair = t if y_pair is None else y_pair + t
            y_pairs.append(y_pair)
        y = y + jnp.concatenate(y_pairs, axis=1) + xs * dskip_ref[:, ch]

        z = z_ref[:, ch].astype(F32)
        gt = y * (z * jax.nn.sigmoid(z))
        o_ref[:, ch] = (_rms(gt, EPS) * nw_ref[:, ch]).astype(o_ref.dtype)


def ssd_branch(proj_a, dt_raw, conv_w, conv_b, dt_bias, a_log, d_skip, ssm_norm, d_ssm, n_groups):
    l = proj_a.shape[0]
    L = ROW_BLOCK
    gw = SSM_GROUP_HEADS * SSM_HEAD_DIM
    n = D_STATE
    n_heads = d_ssm // SSM_HEAD_DIM
    pad_h = L - n_heads
    halo_per_chunk = L // HALO_ROWS

    def halo_idx(c):
        return jnp.maximum(c * halo_per_chunk - 1, 0)

    gps = _pick_tile(n_groups, SSD_GROUPS_PER_STEP)
    sw = gps * gw
    sn = gps * n
    x_col0 = d_ssm // sw
    b_col0 = 2 * d_ssm // sn
    c_col0 = b_col0 + n_groups // gps
    cw_b0 = d_ssm // sn
    cw_c0 = cw_b0 + n_groups // gps
    vec128 = pl.BlockSpec((1, L), lambda g, c: (0, 0))

    def padv(v):
        return jnp.pad(v.astype(F32), (0, pad_h)).reshape(1, L)

    return pl.pallas_call(
        functools.partial(_ssd_kernel, n_heads=n_heads, gps=gps),
        grid=(n_groups // gps, l // L),
        in_specs=[
            pl.BlockSpec((L, sw), lambda g, c: (c, g)),
            pl.BlockSpec((L, sw), lambda g, c: (c, x_col0 + g)),
            pl.BlockSpec((L, sn), lambda g, c: (c, b_col0 + g)),
            pl.BlockSpec((L, sn), lambda g, c: (c, c_col0 + g)),
            pl.BlockSpec((HALO_ROWS, sw), lambda g, c: (halo_idx(c), x_col0 + g)),
            pl.BlockSpec((HALO_ROWS, sn), lambda g, c: (halo_idx(c), b_col0 + g)),
            pl.BlockSpec((HALO_ROWS, sn), lambda g, c: (halo_idx(c), c_col0 + g)),
            pl.BlockSpec((L, L), lambda g, c: (c, 0)),
            pl.BlockSpec((CONV_K, sw), lambda g, c: (0, g)),
            pl.BlockSpec((1, sw), lambda g, c: (0, g)),
            pl.BlockSpec((CONV_K, sn), lambda g, c: (0, cw_b0 + g)),
            pl.BlockSpec((1, sn), lambda g, c: (0, cw_b0 + g)),
            pl.BlockSpec((CONV_K, sn), lambda g, c: (0, cw_c0 + g)),
            pl.BlockSpec((1, sn), lambda g, c: (0, cw_c0 + g)),
            vec128, vec128,
            pl.BlockSpec((1, sw), lambda g, c: (0, g)),
            pl.BlockSpec((1, sw), lambda g, c: (0, g)),
        ],
        out_specs=pl.BlockSpec((L, sw), lambda g, c: (c, g)),
        out_shape=jax.ShapeDtypeStruct((l, d_ssm), BF16),
        scratch_shapes=[pltpu.VMEM((gps, n, gw), F32)],
        compiler_params=_params("parallel", "arbitrary"),
        name="ssd",
    )(proj_a, proj_a, proj_a, proj_a, proj_a, proj_a, proj_a, dt_raw,
      conv_w, conv_b.reshape(1, -1), conv_w, conv_b.reshape(1, -1), conv_w, conv_b.reshape(1, -1),
      padv(dt_bias), padv(a_log),
      jnp.repeat(d_skip.astype(F32), SSM_HEAD_DIM).reshape(1, d_ssm), ssm_norm.reshape(1, d_ssm))


def _lane_tiles(x):
    return [x[:, c * 128:(c + 1) * 128] for c in range(x.shape[1] // 128)]


def _attn_kernel(q_ref, k_ref, v_ref, slope_ref, lq1_ref, lk1_ref, lq2_ref, lk2_ref, subln_ref,
                 o_ref, m_ref, l_ref, acc_ref, sa_ref, sb_ref, *, tq):
    qi = pl.program_id(1)
    d = ATT_HEAD_DIM
    q0 = qi * tq
    slope = slope_ref[0, 0:1, 0:1]
    qpos = q0 + lax.broadcasted_iota(jnp.int32, (tq, 1), 0)

    m_ref[...] = jnp.full(m_ref.shape, NEG_BIG, F32)
    l_ref[...] = jnp.zeros(l_ref.shape, F32)
    acc_ref[...] = jnp.zeros(acc_ref.shape, F32)

    def scores(j, s_ref):
        ks = pl.multiple_of(j * tq, tq)
        kpos = ks + lax.broadcasted_iota(jnp.int32, (1, tq), 1)
        bias = jnp.where(kpos >= LEAD_PAD, slope * (kpos - q0).astype(F32), NEG_BIG)
        for i in range(2):
            s = lax.dot_general(q_ref[:, i * d:(i + 1) * d], k_ref[pl.ds(ks, tq), i * d:(i + 1) * d],
                                (((1,), (1,)), ((), ())), preferred_element_type=F32)
            s_ref[i] = s + bias

    def consume(j, s_ref, masked):
        ks = pl.multiple_of(j * tq, tq)
        vblk = v_ref[pl.ds(ks, tq), :]
        kpos = ks + lax.broadcasted_iota(jnp.int32, (1, tq), 1)
        for i in range(2):
            s = s_ref[i]
            if masked:
                s = jnp.where(kpos <= qpos, s, NEG_BIG)
            m_old = m_ref[i]
            m_new = jnp.maximum(m_old, jnp.max(functools.reduce(jnp.maximum, _lane_tiles(s)), axis=-1, keepdims=True))
            alpha = jnp.exp2(m_old - m_new)
            p = jnp.exp2(s - m_new[:, 0:1])
            l_ref[i] = alpha * l_ref[i] + functools.reduce(jnp.add, _lane_tiles(p))
            acc_ref[i] = (jnp.concatenate([alpha] * (2 * d // 128), axis=1) * acc_ref[i]
                          + jnp.dot(p.astype(BF16), vblk, preferred_element_type=F32))
            m_ref[i] = m_new

    n_chunks = qi + 1
    scores(0, sa_ref)
    bufs = (sa_ref, sb_ref)

    def run(first, count, unroll):
        def body(t, carry):
            for u in range(unroll):
                scores(first + unroll * t + u + 1, bufs[(u + 1) % 2])
                consume(first + unroll * t + u, bufs[u % 2], False)
            return carry
        lax.fori_loop(0, count, body, 0)

    n_long = (n_chunks - 1) // ATT_UNROLL
    run(0, n_long, ATT_UNROLL)
    done = n_long * ATT_UNROLL
    run(done, (n_chunks - done - 1) // 2, 2)

    @pl.when(n_chunks % 2 == 1)
    def _():
        consume(n_chunks - 1, sa_ref, True)

    @pl.when(n_chunks % 2 == 0)
    def _():
        scores(n_chunks - 1, sb_ref)
        consume(n_chunks - 2, sa_ref, False)
        consume(n_chunks - 1, sb_ref, True)

    lam = (jnp.exp(jnp.sum(lq1_ref[...] * lk1_ref[...], axis=-1, keepdims=True))
           - jnp.exp(jnp.sum(lq2_ref[...] * lk2_ref[...], axis=-1, keepdims=True)) + LAMBDA_INIT)
    l0 = jnp.sum(l_ref[0], axis=-1, keepdims=True)
    l1 = jnp.sum(l_ref[1], axis=-1, keepdims=True)
    o = acc_ref[0] / l0 - lam * (acc_ref[1] / l1)
    o = _rms(o, SUBLN_EPS) * subln_ref[...] * (1.0 - LAMBDA_INIT)
    o_ref[...] = o.astype(o_ref.dtype)


def diff_attention(proj_b, l, n_heads, lam_q1, lam_k1, lam_q2, lam_k2, subln):
    hd = 2 * ATT_HEAD_DIM
    assert proj_b.shape[0] == l
    tq = _pick_tile(l, (640, 512, 256, 128))
    slopes = LOG2_E * 2.0 ** (-8.0 * jnp.arange(1, n_heads + 1, dtype=F32) / n_heads)
    slopes = jnp.broadcast_to(slopes[:, None, None], (n_heads, 8, 128))
    vec = pl.BlockSpec((1, ATT_HEAD_DIM), lambda h, i: (0, 0))
    return pl.pallas_call(
        functools.partial(_attn_kernel, tq=tq),
        grid=(n_heads, l // tq),
        in_specs=[
            pl.BlockSpec((tq, hd), lambda h, i: (i, h)),
            pl.BlockSpec((l, hd), lambda h, i: (0, n_heads + h)),
            pl.BlockSpec((l, hd), lambda h, i: (0, 2 * n_heads + h)),
            pl.BlockSpec((1, 8, 128), lambda h, i: (h, 0, 0)),
            vec, vec, vec, vec,
            pl.BlockSpec((1, hd), lambda h, i: (0, 0)),
        ],
        out_specs=pl.BlockSpec((tq, hd), lambda h, i: (i, h)),
        out_shape=jax.ShapeDtypeStruct((l, n_heads * hd), BF16),
        scratch_shapes=[pltpu.VMEM((2, tq, 128), F32), pltpu.VMEM((2, tq, 128), F32), pltpu.VMEM((2, tq, hd), F32),
                        pltpu.VMEM((2, tq, tq), F32), pltpu.VMEM((2, tq, tq), F32)],
        compiler_params=_params("parallel", "arbitrary"),
        name="diff_attention",
    )(proj_b, proj_b, proj_b, slopes,
      lam_q1.reshape(1, -1), lam_k1.reshape(1, -1), lam_q2.reshape(1, -1), lam_k2.reshape(1, -1),
      subln.reshape(1, hd))


def kernel(x, meta_tokens, norm_pre_mix, w_in, conv_w, conv_b, dt_bias, a_log, d_skip, ssm_norm, w_ssm_out,
           lam_q1, lam_k1, lam_q2, lam_k2, subln, w_att_out, w_o, norm_post_mix, norm_pre_mlp, w_ff1, w_ff2,
           norm_post_mlp):
    batch, seq, d_model = x.shape
    assert batch == 1 and w_in.shape[0] == 1 and seq % ROW_BLOCK == 0
    d_ssm = w_ssm_out.shape[1]
    d_att = w_att_out.shape[1]
    n_ssm_heads = d_ssm // SSM_HEAD_DIM
    n_groups = n_ssm_heads // SSM_GROUP_HEADS
    n_att_heads = d_att // (2 * ATT_HEAD_DIM)
    assert n_ssm_heads <= ROW_BLOCK
    n_a = 2 * d_ssm + 2 * n_groups * D_STATE
    n_dt = n_ssm_heads
    assert w_in.shape[2] == n_a + n_dt + 3 * d_att + 2 * d_model
    l = ROW_BLOCK + seq
    meta = meta_tokens.astype(F32)

    u = rms_cast(x[0], meta, norm_pre_mix[0], l)
    w_in_t = jnp.swapaxes(w_in, 1, 2)
    proj_a = matmul(u, w_in_t, BF16, m=l, n=n_a, transposed=True, name="in_proj_ssm")
    dt_raw = matmul(u, w_in_t, F32, m=l, col0=n_a, n=ROW_BLOCK, transposed=True, name="in_proj_dt")
    q_scale = ATT_HEAD_DIM ** -0.5 * LOG2_E
    n_b = 3 * d_att + 2 * d_model
    tn_b = _pick_tile(math.gcd(n_b, n_a), (512, 256, 128))
    proj_b = matmul(u, w_in_t, BF16, col0=n_a + n_dt, n=n_b, transposed=True, name="in_proj_att",
                    epilogue=lambda acc, j: acc * jnp.where(j < d_att // tn_b, q_scale, 1.0))

    y_ssm = ssd_branch(proj_a, dt_raw, conv_w[0], conv_b[0], dt_bias[0], a_log[0], d_skip[0], ssm_norm[0],
                       d_ssm, n_groups)
    y_att = diff_attention(proj_b, l, n_att_heads, lam_q1[0], lam_k1[0], lam_q2[0], lam_k2[0], subln[0])

    mixed = gated_merge(y_ssm, y_att, w_ssm_out, w_att_out, proj_b, 3 * d_att)
    t = matmul(mixed, w_o, BF16, name="out_proj")
    h1, u2 = post_mix(x[0], meta, t, norm_post_mix[0], norm_pre_mlp[0])

    f1 = matmul(u2, w_ff1, BF16, epilogue=lambda acc, j: jnp.square(jnp.maximum(acc, 0.0)), name="ff1")
    f2 = matmul_long_k(f1, w_ff2, BF16, name="ff2")
    out = post_mlp(h1, f2, norm_post_mlp[0], seq)
    return out[None]
```

```python
import functools
import math

import jax
import jax.numpy as jnp
from jax import lax
from jax.experimental import pallas as pl
from jax.experimental.pallas import tpu as pltpu

F32 = jnp.float32
BF16 = jnp.bfloat16

LANES = 128
SUBLANES = 8
N_META = 16
ROW_BLOCK = 128
LEAD_PAD = ROW_BLOCK - N_META
SSM_HEAD_DIM = 64
SSM_GROUP_HEADS = 8
D_STATE = 128
CONV_K = 4
ATT_HEAD_DIM = 128
EPS = 1e-6
SUBLN_EPS = 1e-5
NEG_BIG = -1e30
LOG2_E = math.log2(math.e)
LAMBDA_INIT = 0.8 - 0.6 * math.exp(-0.3 * 0)
HALO_ROWS = 16
SSD_GROUPS_PER_STEP = (8, 4, 2, 1)
ATT_UNROLL = 4

VMEM_LIMIT_BYTES = 56 * 1024 * 1024


def _params(*sem):
    return pltpu.CompilerParams(dimension_semantics=sem, vmem_limit_bytes=VMEM_LIMIT_BYTES)


def _pick_tile(n, candidates):
    for c in candidates:
        if n % c == 0:
            return c
    return n


M_TILES = (2080, 1664, 1408, 1280, 1024, 640, 512, 256, 128)
M_TILES_2X = (1664, 1408, 1280, 1024, 640, 512, 256, 128)
M_TILES_STREAM = (640, 512, 256, 128)


def _rms(x, eps):
    return x * lax.rsqrt(jnp.mean(x * x, axis=-1, keepdims=True) + eps)


def _h_block(i, x_ref, meta_ref):
    lead = jnp.concatenate([jnp.zeros((LEAD_PAD, meta_ref.shape[1]), F32), meta_ref[...]], axis=0)
    return jnp.where(i == 0, lead, x_ref[...])


def _h_specs(d):
    return [pl.BlockSpec((ROW_BLOCK, d), lambda i: (jnp.maximum(i - 1, 0), 0)),
            pl.BlockSpec((N_META, d), lambda i: (0, 0))]


def _rms_cast_kernel(x_ref, meta_ref, w_ref, o_ref):
    h = _h_block(pl.program_id(0), x_ref, meta_ref)
    o_ref[...] = (_rms(h, EPS) * w_ref[...]).astype(o_ref.dtype)


def rms_cast(x, meta, w):
    seq, d = x.shape
    rows = ROW_BLOCK + seq
    return pl.pallas_call(
        _rms_cast_kernel,
        grid=(rows // ROW_BLOCK,),
        in_specs=_h_specs(d) + [pl.BlockSpec((1, d), lambda i: (0, 0))],
        out_specs=pl.BlockSpec((ROW_BLOCK, d), lambda i: (i, 0)),
        out_shape=jax.ShapeDtypeStruct((rows, d), BF16),
        compiler_params=_params("parallel"),
        name="rms_cast",
    )(x, meta, w.reshape(1, d))


def _post_mix_kernel(x_ref, meta_ref, t_ref, wpost_ref, wpre_ref, h1_ref, u_ref):
    h1 = _h_block(pl.program_id(0), x_ref, meta_ref) + _rms(t_ref[...].astype(F32), EPS) * wpost_ref[...]
    h1_ref[...] = h1
    u_ref[...] = (_rms(h1, EPS) * wpre_ref[...]).astype(u_ref.dtype)


def post_mix(x, meta, t, w_post, w_pre):
    seq, d = x.shape
    l = t.shape[0]
    row = pl.BlockSpec((ROW_BLOCK, d), lambda i: (i, 0))
    vec = pl.BlockSpec((1, d), lambda i: (0, 0))
    return pl.pallas_call(
        _post_mix_kernel,
        grid=(l // ROW_BLOCK,),
        in_specs=_h_specs(d) + [row, vec, vec],
        out_specs=[row, row],
        out_shape=[jax.ShapeDtypeStruct((l, d), F32), jax.ShapeDtypeStruct((l, d), BF16)],
        compiler_params=_params("parallel"),
        name="post_mix",
    )(x, meta, t, w_post.reshape(1, d), w_pre.reshape(1, d))


def _post_mlp_kernel(h_ref, f_ref, w_ref, o_ref):
    o_ref[...] = h_ref[...] + _rms(f_ref[...].astype(F32), EPS) * w_ref[...]


def post_mlp(h1, f, w, seq):
    l, d = h1.shape
    tm = ROW_BLOCK
    skip = (l - seq) // tm
    row_in = pl.BlockSpec((tm, d), lambda i: (i + skip, 0))
    return pl.pallas_call(
        _post_mlp_kernel,
        grid=(seq // tm,),
        in_specs=[row_in, row_in, pl.BlockSpec((1, d), lambda i: (0, 0))],
        out_specs=pl.BlockSpec((tm, d), lambda i: (i, 0)),
        out_shape=jax.ShapeDtypeStruct((seq, d), F32),
        compiler_params=_params("parallel"),
        name="post_mlp",
    )(h1, f, w.reshape(1, d))


def _mm_kernel(x_ref, w_ref, *rest, epilogue, shift, transposed):
    *rest, o_ref = rest
    if shift:
        wnext_ref, *rest = rest
        if transposed:
            w = jnp.concatenate([w_ref[shift:, :].astype(BF16), wnext_ref[:shift, :].astype(BF16)], axis=0)
        else:
            w = jnp.concatenate([w_ref[:, shift:].astype(BF16), wnext_ref[:, :shift].astype(BF16)], axis=1)
    else:
        w = w_ref[...].astype(BF16)
    contract = (((1,), (1,)), ((), ())) if transposed else (((1,), (0,)), ((), ()))
    acc = lax.dot_general(x_ref[...], w, contract, preferred_element_type=F32)
    side = [r[...].astype(F32) for r in rest]
    o_ref[...] = epilogue(acc, pl.program_id(1), *side).astype(o_ref.dtype)


def matmul(x, w, out_dtype, *, col0=0, n=None, transposed=False, epilogue=lambda acc, j: acc, side_inputs=(),
           name="matmul"):
    m, k = x.shape
    n = w.shape[1 if transposed else 2] - col0 if n is None else n
    shift = col0 % LANES
    base = col0 - shift
    tn = _pick_tile(math.gcd(n, base) if base else n, (512, 256, 128))
    cb0 = base // tn
    next_blk = tn // LANES
    if transposed:
        w_spec = pl.BlockSpec((None, tn, k), lambda i, j: (0, cb0 + j, 0))
        w_next_spec = pl.BlockSpec((None, LANES, k), lambda i, j: (0, (cb0 + j + 1) * next_blk, 0))
    else:
        w_spec = pl.BlockSpec((None, k, tn), lambda i, j: (0, 0, cb0 + j))
        w_next_spec = pl.BlockSpec((None, k, LANES), lambda i, j: (0, 0, (cb0 + j + 1) * next_blk))
    if n == tn:
        tm = _pick_tile(m, M_TILES_STREAM)
        x_spec = pl.BlockSpec((tm, k), lambda i, j: (i, 0))
    else:
        tm = _pick_tile(m, M_TILES)
        x_spec = pl.BlockSpec((tm, k), lambda i, j: (i, 0), pipeline_mode=pl.Buffered(1))
    in_specs = [x_spec, w_spec]
    operands = [x, w]
    if shift:
        in_specs.append(w_next_spec)
        operands.append(w)
    for arr, side_col0 in side_inputs:
        in_specs.append(pl.BlockSpec((tm, tn), lambda i, j, _b=side_col0 // tn: (i, _b + j)))
        operands.append(arr)
    return pl.pallas_call(
        functools.partial(_mm_kernel, epilogue=epilogue, shift=shift, transposed=transposed),
        grid=(m // tm, n // tn),
        in_specs=in_specs,
        out_specs=pl.BlockSpec((tm, tn), lambda i, j: (i, j)),
        out_shape=jax.ShapeDtypeStruct((m, n), out_dtype),
        compiler_params=_params("parallel", "arbitrary"),
        name=name,
    )(*operands)


def _mm_acc_kernel(x_ref, w_ref, o_ref, acc_ref):
    kk = pl.program_id(2)

    @pl.when(kk == 0)
    def _():
        acc_ref[...] = jnp.zeros(acc_ref.shape, F32)

    acc_ref[...] += jnp.dot(x_ref[...], w_ref[...].astype(BF16), preferred_element_type=F32)

    @pl.when(kk == pl.num_programs(2) - 1)
    def _():
        o_ref[...] = acc_ref[...].astype(o_ref.dtype)


def matmul_long_k(x, w, out_dtype, name="matmul_long_k"):
    m, k = x.shape
    n = w.shape[2]
    tm = _pick_tile(m, M_TILES_2X)
    tn = _pick_tile(n, (1024, 512, 256, 128))
    tk = _pick_tile(k, (2048, 1024, 512))
    return pl.pallas_call(
        _mm_acc_kernel,
        grid=(m // tm, n // tn, k // tk),
        in_specs=[pl.BlockSpec((tm, tk), lambda i, j, kk: (i, kk)),
                  pl.BlockSpec((None, tk, tn), lambda i, j, kk: (0, kk, j))],
        out_specs=pl.BlockSpec((tm, tn), lambda i, j, kk: (i, j)),
        out_shape=jax.ShapeDtypeStruct((m, n), out_dtype),
        scratch_shapes=[pltpu.VMEM((tm, tn), F32)],
        compiler_params=_params("parallel", "arbitrary", "arbitrary"),
        name=name,
    )(x, w)


def gated_merge(y_ssm, y_att, w_ssm, w_att, proj_b, gate_col0):
    n = w_ssm.shape[2]
    part = matmul(y_ssm, w_ssm, BF16, side_inputs=[(proj_b, gate_col0)], name="merge_ssm",
                  epilogue=lambda acc, j, gate: jax.nn.sigmoid(gate) * acc)
    return matmul(y_att, w_att, BF16, side_inputs=[(proj_b, gate_col0 + n), (part, 0)], name="merge_att",
                  epilogue=lambda acc, j, gate, prev: prev + jax.nn.sigmoid(gate) * acc)


def _split_bf16(x, parts):
    out = []
    r = x
    for _ in range(parts):
        p = r.astype(BF16)
        out.append(p)
        r = r - p.astype(F32)
    return out


def _dot_onehot(x, onehot, parts):
    return sum(jnp.dot(p, onehot, preferred_element_type=F32) for p in _split_bf16(x, parts))


def _causal_conv_silu(cur_ref, halo_ref, w_ref, b_ref, first_chunk):
    cur = cur_ref[...].astype(F32)
    halo = halo_ref[...].astype(F32)[HALO_ROWS - 8:, :]
    halo = jnp.where(first_chunk, 0.0, halo)
    row8 = lax.broadcasted_iota(jnp.int32, (8, 1), 0)
    acc = b_ref[...] + w_ref[CONV_K - 1:CONV_K, :] * cur
    for shift in range(1, CONV_K):
        rolled = pltpu.roll(cur, shift, 0)
        top = jnp.where(row8 < shift, pltpu.roll(halo, shift, 0), rolled[0:8, :])
        shifted = jnp.concatenate([top, rolled[8:, :]], axis=0)
        acc = acc + w_ref[CONV_K - 1 - shift:CONV_K - shift, :] * shifted
    return acc * jax.nn.sigmoid(acc)


def _ssd_kernel(z_ref, xs_ref, b_ref, c_ref, xsh_ref, bh_ref, ch_ref, dt_ref,
                cwx_ref, cbx_ref, cwb_ref, cbb_ref, cwc_ref, cbc_ref,
                dtb_ref, alog_ref, dskip_ref, nw_ref, o_ref, state_ref, *, n_heads, gps):
    g0 = pl.program_id(0) * gps
    c = pl.program_id(1)
    L = ROW_BLOCK
    gw = SSM_GROUP_HEADS * SSM_HEAD_DIM
    n = D_STATE
    first = c == 0

    @pl.when(first)
    def _():
        state_ref[...] = jnp.zeros_like(state_ref)

    xs_all = _causal_conv_silu(xs_ref, xsh_ref, cwx_ref, cbx_ref, first)
    bm_all = _causal_conv_silu(b_ref, bh_ref, cwb_ref, cbb_ref, first)
    cm_all = _causal_conv_silu(c_ref, ch_ref, cwc_ref, cbc_ref, first)

    dt = jax.nn.softplus(dt_ref[...] + dtb_ref[...])
    row = c * L + lax.broadcasted_iota(jnp.int32, (L, 1), 0)
    lane = lax.broadcasted_iota(jnp.int32, (1, L), 1)
    dt = jnp.where((row >= LEAD_PAD) & (lane < n_heads), dt, 0.0)
    adt = dt * (-jnp.exp(alog_ref[...]))
    r_i = lax.broadcasted_iota(jnp.int32, (L, L), 0)
    c_i = lax.broadcasted_iota(jnp.int32, (L, L), 1)
    causal = r_i >= c_i
    tri = causal.astype(BF16)
    adt_parts = _split_bf16(adt, 3)
    a_cum = sum(jnp.dot(tri, p, preferred_element_type=F32) for p in adt_parts)
    n_pick = max(HALO_ROWS, gps * SSM_GROUP_HEADS)
    pick = (g0 * SSM_GROUP_HEADS + lax.broadcasted_iota(jnp.int32, (n_pick, L), 0)
            == lax.broadcasted_iota(jnp.int32, (n_pick, L), 1)).astype(BF16)
    a_t = sum(lax.dot_general(pick, p, (((1,), (1,)), ((), ())), preferred_element_type=F32)
              for p in _split_bf16(a_cum, 3))

    hl = lax.broadcasted_iota(jnp.int32, (L, gps * gw), 0)
    ch_head = g0 * SSM_GROUP_HEADS + lax.broadcasted_iota(jnp.int32, (L, gps * gw), 1) // SSM_HEAD_DIM
    spread = (hl == ch_head).astype(BF16)
    wide = gps * SSM_GROUP_HEADS * L
    hl2 = lax.broadcasted_iota(jnp.int32, (L, wide), 0)
    col_head = g0 * SSM_GROUP_HEADS + lax.broadcasted_iota(jnp.int32, (L, wide), 1) // L
    spread_wide = (hl2 == col_head).astype(BF16)

    a_ch_all = _dot_onehot(a_cum, spread, 3)
    dt_ch_all = _dot_onehot(dt, spread, 3)
    a_col_all = _dot_onehot(a_cum, spread_wide, 3)

    lane_lo = lax.broadcasted_iota(jnp.int32, (1, 2 * SSM_HEAD_DIM), 1) < SSM_HEAD_DIM
    for gg in range(gps):
        ch = slice(gg * gw, (gg + 1) * gw)
        st = slice(gg * n, (gg + 1) * n)
        xs, a_ch = xs_all[:, ch], a_ch_all[:, ch]
        xdt = xs * dt_ch_all[:, ch]
        a_last = a_ch[L - 1:L, :]
        bm = bm_all[:, st]
        cm_b = cm_all[:, st].astype(BF16)
        cb = lax.dot_general(cm_b, bm.astype(BF16), (((1,), (1,)), ((), ())), preferred_element_type=F32)

        state = state_ref[gg]
        y = jnp.dot(cm_b, state.astype(BF16), preferred_element_type=F32) * jnp.exp(a_ch)
        decayed = (xdt * jnp.exp(a_last - a_ch)).astype(BF16)
        state_ref[gg] = state * jnp.exp(a_last) + jnp.dot(bm.T.astype(BF16), decayed, preferred_element_type=F32)

        xdt_b = xdt.astype(BF16)
        y_pairs = []
        for pp in range(SSM_GROUP_HEADS // 2):
            x_pair = xdt_b[:, pp * 2 * SSM_HEAD_DIM:(pp + 1) * 2 * SSM_HEAD_DIM]
            y_pair = None
            for half in range(2):
                hh = gg * SSM_GROUP_HEADS + 2 * pp + half
                seg = a_col_all[:, hh * L:(hh + 1) * L] - a_t[hh:hh + 1, :]
                decay = jnp.exp(jnp.where(causal, seg, -jnp.inf))
                lhs = (cb * decay).astype(BF16)
                rhs = jnp.where(lane_lo if half == 0 else jnp.logical_not(lane_lo), x_pair, jnp.zeros_like(x_pair))
                t = jnp.dot(lhs, rhs, preferred_element_type=F32)
                y_pair = t if y_pair is None else y_pair + t
            y_pairs.append(y_pair)
        y = y + jnp.concatenate(y_pairs, axis=1) + xs * dskip_ref[:, ch]

        z = z_ref[:, ch].astype(F32)
        gt = y * (z * jax.nn.sigmoid(z))
        o_ref[:, ch] = (_rms(gt, EPS) * nw_ref[:, ch]).astype(o_ref.dtype)


def ssd_branch(proj_a, dt_raw, conv_w, conv_b, dt_bias, a_log, d_skip, ssm_norm, d_ssm, n_groups):
    l = proj_a.shape[0]
    L = ROW_BLOCK
    gw = SSM_GROUP_HEADS * SSM_HEAD_DIM
    n = D_STATE
    n_heads = d_ssm // SSM_HEAD_DIM
    pad_h = L - n_heads
    halo_per_chunk = L // HALO_ROWS

    def halo_idx(c):
        return jnp.maximum(c * halo_per_chunk - 1, 0)

    gps = _pick_tile(n_groups, SSD_GROUPS_PER_STEP)
    sw = gps * gw
    sn = gps * n
    x_col0 = d_ssm // sw
    b_col0 = 2 * d_ssm // sn
    c_col0 = b_col0 + n_groups // gps
    cw_b0 = d_ssm // sn
    cw_c0 = cw_b0 + n_groups // gps
    vec128 = pl.BlockSpec((1, L), lambda g, c: (0, 0))

    def padv(v):
        return jnp.pad(v.astype(F32), (0, pad_h)).reshape(1, L)

    return pl.pallas_call(
        functools.partial(_ssd_kernel, n_heads=n_heads, gps=gps),
        grid=(n_groups // gps, l // L),
        in_specs=[
            pl.BlockSpec((L, sw), lambda g, c: (c, g)),
            pl.BlockSpec((L, sw), lambda g, c: (c, x_col0 + g)),
            pl.BlockSpec((L, sn), lambda g, c: (c, b_col0 + g)),
            pl.BlockSpec((L, sn), lambda g, c: (c, c_col0 + g)),
            pl.BlockSpec((HALO_ROWS, sw), lambda g, c: (halo_idx(c), x_col0 + g)),
            pl.BlockSpec((HALO_ROWS, sn), lambda g, c: (halo_idx(c), b_col0 + g)),
            pl.BlockSpec((HALO_ROWS, sn), lambda g, c: (halo_idx(c), c_col0 + g)),
            pl.BlockSpec((L, L), lambda g, c: (c, 0)),
            pl.BlockSpec((CONV_K, sw), lambda g, c: (0, g)),
            pl.BlockSpec((1, sw), lambda g, c: (0, g)),
            pl.BlockSpec((CONV_K, sn), lambda g, c: (0, cw_b0 + g)),
            pl.BlockSpec((1, sn), lambda g, c: (0, cw_b0 + g)),
            pl.BlockSpec((CONV_K, sn), lambda g, c: (0, cw_c0 + g)),
            pl.BlockSpec((1, sn), lambda g, c: (0, cw_c0 + g)),
            vec128, vec128,
            pl.BlockSpec((1, sw), lambda g, c: (0, g)),
            pl.BlockSpec((1, sw), lambda g, c: (0, g)),
        ],
        out_specs=pl.BlockSpec((L, sw), lambda g, c: (c, g)),
        out_shape=jax.ShapeDtypeStruct((l, d_ssm), BF16),
        scratch_shapes=[pltpu.VMEM((gps, n, gw), F32)],
        compiler_params=_params("parallel", "arbitrary"),
        name="ssd",
    )(proj_a, proj_a, proj_a, proj_a, proj_a, proj_a, proj_a, dt_raw,
      conv_w, conv_b.reshape(1, -1), conv_w, conv_b.reshape(1, -1), conv_w, conv_b.reshape(1, -1),
      padv(dt_bias), padv(a_log),
      jnp.repeat(d_skip.astype(F32), SSM_HEAD_DIM).reshape(1, d_ssm), ssm_norm.reshape(1, d_ssm))


def _lane_tiles(x):
    return [x[:, c * LANES:(c + 1) * LANES] for c in range(x.shape[1] // LANES)]


def _attn_kernel(q_ref, k_ref, v_ref, slope_ref, lq1_ref, lk1_ref, lq2_ref, lk2_ref, subln_ref,
                 o_ref, m_ref, l_ref, acc_ref, sa_ref, sb_ref, *, tq):
    qi = pl.program_id(1)
    d = ATT_HEAD_DIM
    q0 = qi * tq
    slope = slope_ref[0, 0:1, 0:1]
    qpos = q0 + lax.broadcasted_iota(jnp.int32, (tq, 1), 0)

    m_ref[...] = jnp.full(m_ref.shape, NEG_BIG, F32)
    l_ref[...] = jnp.zeros(l_ref.shape, F32)
    acc_ref[...] = jnp.zeros(acc_ref.shape, F32)

    def scores(j, s_ref):
        ks = pl.multiple_of(j * tq, tq)
        kpos = ks + lax.broadcasted_iota(jnp.int32, (1, tq), 1)
        bias = jnp.where(kpos >= LEAD_PAD, slope * (kpos - q0).astype(F32), NEG_BIG)
        for i in range(2):
            s = lax.dot_general(q_ref[:, i * d:(i + 1) * d], k_ref[pl.ds(ks, tq), i * d:(i + 1) * d],
                                (((1,), (1,)), ((), ())), preferred_element_type=F32)
            s_ref[i] = s + bias

    def consume(j, s_ref, masked):
        ks = pl.multiple_of(j * tq, tq)
        vblk = v_ref[pl.ds(ks, tq), :]
        kpos = ks + lax.broadcasted_iota(jnp.int32, (1, tq), 1)
        for i in range(2):
            s = s_ref[i]
            if masked:
                s = jnp.where(kpos <= qpos, s, NEG_BIG)
            m_old = m_ref[i]
            m_new = jnp.maximum(m_old, jnp.max(functools.reduce(jnp.maximum, _lane_tiles(s)), axis=-1, keepdims=True))
            alpha = jnp.exp2(m_old - m_new)
            p = jnp.exp2(s - m_new[:, 0:1])
            l_ref[i] = alpha * l_ref[i] + functools.reduce(jnp.add, _lane_tiles(p))
            acc_ref[i] = (jnp.concatenate([alpha] * (2 * d // LANES), axis=1) * acc_ref[i]
                          + jnp.dot(p.astype(BF16), vblk, preferred_element_type=F32))
            m_ref[i] = m_new

    n_chunks = qi + 1
    scores(0, sa_ref)
    bufs = (sa_ref, sb_ref)

    def run(first, count, unroll):
        def body(t, carry):
            for u in range(unroll):
                scores(first + unroll * t + u + 1, bufs[(u + 1) % 2])
                consume(first + unroll * t + u, bufs[u % 2], False)
            return carry
        lax.fori_loop(0, count, body, 0)

    n_long = (n_chunks - 1) // ATT_UNROLL
    run(0, n_long, ATT_UNROLL)
    done = n_long * ATT_UNROLL
    run(done, (n_chunks - done - 1) // 2, 2)

    @pl.when(n_chunks % 2 == 1)
    def _():
        consume(n_chunks - 1, sa_ref, True)

    @pl.when(n_chunks % 2 == 0)
    def _():
        scores(n_chunks - 1, sb_ref)
        consume(n_chunks - 2, sa_ref, False)
        consume(n_chunks - 1, sb_ref, True)

    lam = (jnp.exp(jnp.sum(lq1_ref[...] * lk1_ref[...], axis=-1, keepdims=True))
           - jnp.exp(jnp.sum(lq2_ref[...] * lk2_ref[...], axis=-1, keepdims=True)) + LAMBDA_INIT)
    l0 = jnp.sum(l_ref[0], axis=-1, keepdims=True)
    l1 = jnp.sum(l_ref[1], axis=-1, keepdims=True)
    o = acc_ref[0] / l0 - lam * (acc_ref[1] / l1)
    o = _rms(o, SUBLN_EPS) * subln_ref[...] * (1.0 - LAMBDA_INIT)
    o_ref[...] = o.astype(o_ref.dtype)


def diff_attention(proj_b, l, n_heads, lam_q1, lam_k1, lam_q2, lam_k2, subln):
    hd = 2 * ATT_HEAD_DIM
    assert proj_b.shape[0] == l
    tq = _pick_tile(l, (640, 512, 256, 128))
    slopes = LOG2_E * 2.0 ** (-8.0 * jnp.arange(1, n_heads + 1, dtype=F32) / n_heads)
    slopes = jnp.broadcast_to(slopes[:, None, None], (n_heads, SUBLANES, LANES))
    vec = pl.BlockSpec((1, ATT_HEAD_DIM), lambda h, i: (0, 0))
    return pl.pallas_call(
        functools.partial(_attn_kernel, tq=tq),
        grid=(n_heads, l // tq),
        in_specs=[
            pl.BlockSpec((tq, hd), lambda h, i: (i, h)),
            pl.BlockSpec((l, hd), lambda h, i: (0, n_heads + h)),
            pl.BlockSpec((l, hd), lambda h, i: (0, 2 * n_heads + h)),
            pl.BlockSpec((1, SUBLANES, LANES), lambda h, i: (h, 0, 0)),
            vec, vec, vec, vec,
            pl.BlockSpec((1, hd), lambda h, i: (0, 0)),
        ],
        out_specs=pl.BlockSpec((tq, hd), lambda h, i: (i, h)),
        out_shape=jax.ShapeDtypeStruct((l, n_heads * hd), BF16),
        scratch_shapes=[pltpu.VMEM((2, tq, LANES), F32), pltpu.VMEM((2, tq, LANES), F32), pltpu.VMEM((2, tq, hd), F32),
                        pltpu.VMEM((2, tq, tq), F32), pltpu.VMEM((2, tq, tq), F32)],
        compiler_params=_params("parallel", "arbitrary"),
        name="diff_attention",
    )(proj_b, proj_b, proj_b, slopes,
      lam_q1.reshape(1, -1), lam_k1.reshape(1, -1), lam_q2.reshape(1, -1), lam_k2.reshape(1, -1),
      subln.reshape(1, hd))


def kernel(x, meta_tokens, norm_pre_mix, w_in, conv_w, conv_b, dt_bias, a_log, d_skip, ssm_norm, w_ssm_out,
           lam_q1, lam_k1, lam_q2, lam_k2, subln, w_att_out, w_o, norm_post_mix, norm_pre_mlp, w_ff1, w_ff2,
           norm_post_mlp):
    batch, seq, d_model = x.shape
    assert batch == 1 and w_in.shape[0] == 1 and seq % ROW_BLOCK == 0
    d_ssm = w_ssm_out.shape[1]
    d_att = w_att_out.shape[1]
    n_ssm_heads = d_ssm // SSM_HEAD_DIM
    n_groups = n_ssm_heads // SSM_GROUP_HEADS
    n_att_heads = d_att // (2 * ATT_HEAD_DIM)
    assert n_ssm_heads <= ROW_BLOCK
    n_a = 2 * d_ssm + 2 * n_groups * D_STATE
    n_dt = n_ssm_heads
    assert w_in.shape[2] == n_a + n_dt + 3 * d_att + 2 * d_model
    l = ROW_BLOCK + seq
    meta = meta_tokens.astype(F32)

    u = rms_cast(x[0], meta, norm_pre_mix[0])
    w_in_t = jnp.swapaxes(w_in, 1, 2)
    proj_a = matmul(u, w_in_t, BF16, n=n_a, transposed=True, name="in_proj_ssm")
    dt_raw = matmul(u, w_in_t, F32, col0=n_a, n=ROW_BLOCK, transposed=True, name="in_proj_dt")
    q_scale = ATT_HEAD_DIM ** -0.5 * LOG2_E
    n_b = 3 * d_att + 2 * d_model
    tn_b = _pick_tile(math.gcd(n_b, n_a), (512, 256, 128))
    proj_b = matmul(u, w_in_t, BF16, col0=n_a + n_dt, n=n_b, transposed=True, name="in_proj_att",
                    epilogue=lambda acc, j: acc * jnp.where(j < d_att // tn_b, q_scale, 1.0))

    y_ssm = ssd_branch(proj_a, dt_raw, conv_w[0], conv_b[0], dt_bias[0], a_log[0], d_skip[0], ssm_norm[0],
                       d_ssm, n_groups)
    y_att = diff_attention(proj_b, l, n_att_heads, lam_q1[0], lam_k1[0], lam_q2[0], lam_k2[0], subln[0])

    mixed = gated_merge(y_ssm, y_att, w_ssm_out, w_att_out, proj_b, 3 * d_att)
    t = matmul(mixed, w_o, BF16, name="out_proj")
    h1, u2 = post_mix(x[0], meta, t, norm_post_mix[0], norm_pre_mlp[0])

    f1 = matmul(u2, w_ff1, BF16, epilogue=lambda acc, j: jnp.square(jnp.maximum(acc, 0.0)), name="ff1")
    f2 = matmul_long_k(f1, w_ff2, BF16, name="ff2")
    out = post_mlp(h1, f2, norm_post_mlp[0], seq)
    return out[None]
```
